```python
import math
import jax, jax.numpy as jnp
from jax import lax
import numpy as np

D_MODEL = 1024
BATCH = 2
SEQ = 8192
DEPTH = 1

N_Q_HEADS = 8
N_KV_HEADS = 2
HEAD_DIM = 64
ATTN_WIDTH = N_Q_HEADS * HEAD_DIM
KV_WIDTH = N_KV_HEADS * HEAD_DIM
WINDOW = 128
BLOCK = 128
POOL_WINDOWS = (2, 4, 8, 16)
N_POOL_GROUPS = len(POOL_WINDOWS)
POOL_GROUP_WIDTH = 128
POOL_WIDTH = N_POOL_GROUPS * POOL_GROUP_WIDTH
N_BRANCHES = 2
IN_WIDTH = ATTN_WIDTH + 2 * KV_WIDTH + POOL_WIDTH + N_BRANCHES * D_MODEL
N_EXPERTS = 32
TOP_K = 4
D_EXPERT = D_MODEL
SWIGLU_LIMIT = 7.0
SWIGLU_ALPHA = 1.702
MOE_BLOCK = 128
LN_EPS = 1e-5
DEEPNORM_ALPHA = (2.0 * DEPTH) ** 0.25
DEEPNORM_BETA = (8.0 * DEPTH) ** -0.25

kernel_name = 'hybrid_swa_pool_moe_deepnorm_encoder'


def layer_norm(x, g, b):
    xf = x.astype(jnp.float32)
    mean = jnp.mean(xf, axis=-1, keepdims=True)
    var = jnp.mean(jnp.square(xf - mean), axis=-1, keepdims=True)
    y = (xf - mean) * lax.rsqrt(var + LN_EPS)
    return y.astype(x.dtype) * g + b


def alibi_slopes(n_heads):
    return np.asarray([2.0 ** (-8.0 * (h + 1) / n_heads) for h in range(n_heads)], dtype=np.float32)


def windowed_attention(q, k, v, sinks):
    B, S = q.shape[0], q.shape[1]
    nb = S // BLOCK
    G = N_Q_HEADS // N_KV_HEADS
    qb = q.reshape(B, nb, BLOCK, N_KV_HEADS, G, HEAD_DIM)
    pad = ((0, 0), (BLOCK, BLOCK), (0, 0), (0, 0))
    kp = jnp.pad(k, pad).reshape(B, nb + 2, BLOCK, N_KV_HEADS, HEAD_DIM)
    vp = jnp.pad(v, pad).reshape(B, nb + 2, BLOCK, N_KV_HEADS, HEAD_DIM)
    kw = jnp.concatenate([kp[:, :-2], kp[:, 1:-1], kp[:, 2:]], axis=2)
    vw = jnp.concatenate([vp[:, :-2], vp[:, 1:-1], vp[:, 2:]], axis=2)
    scores = jnp.einsum('bnqhgd,bnkhd->bnhgqk', qb, kw).astype(jnp.float32)
    scores = scores * (1.0 / math.sqrt(HEAD_DIM))
    rel = np.arange(3 * BLOCK)[None, :] - BLOCK - np.arange(BLOCK)[:, None]
    key_abs = np.arange(nb)[:, None] * BLOCK - BLOCK + np.arange(3 * BLOCK)[None, :]
    valid = (np.abs(rel) <= WINDOW)[None, :, :] & ((key_abs >= 0) & (key_abs < S))[:, None, :]
    slopes = jnp.asarray(alibi_slopes(N_Q_HEADS)).reshape(N_KV_HEADS, G)
    bias = -slopes[:, :, None, None] * jnp.asarray(np.abs(rel), jnp.float32)[None, None]
    scores = jnp.where(jnp.asarray(valid)[None, :, None, None], scores + bias[None, None], -jnp.inf)
    sink = sinks.astype(jnp.float32).reshape(N_KV_HEADS, G)[None, None, :, :, None, None]
    m = jnp.maximum(jnp.max(scores, axis=-1, keepdims=True), sink)
    p = jnp.exp(scores - m)
    probs = p / (jnp.sum(p, axis=-1, keepdims=True) + jnp.exp(sink - m))
    out = jnp.einsum('bnhgqk,bnkhd->bnqhgd', probs.astype(v.dtype), vw)
    return out.reshape(B, S, ATTN_WIDTH)


def multiscale_pool(p, w_group, scale):
    B, S = p.shape[0], p.shape[1]
    cs = jnp.concatenate([jnp.zeros((B, 1, POOL_WIDTH), jnp.float32),
                          jnp.cumsum(p.astype(jnp.float32), axis=1)], axis=1)
    t = np.arange(S)
    outs = []
    for g, w in enumerate(POOL_WINDOWS):
        sl = slice(g * POOL_GROUP_WIDTH, (g + 1) * POOL_GROUP_WIDTH)
        lo = np.maximum(t - w // 2, 0)
        hi = np.minimum(t + w // 2 - 1, S - 1)
        cnt = jnp.asarray((hi - lo + 1).astype(np.float32))[None, :, None]
        csg = cs[:, :, sl]
        pooled = (csg[:, hi + 1] - csg[:, lo]) / cnt
        outs.append(pooled.astype(p.dtype) - p[:, :, sl])
    y = jnp.stack(outs, axis=2)
    y = jnp.einsum('bsgc,gcd->bsgd', y, w_group).reshape(B, S, POOL_WIDTH)
    return y * scale


def moe_ffn(x, w_router, b_router, w_mlp1, b_mlp1, w_mlp2, b_mlp2):
    B, S, D = x.shape
    N = B * S
    A = N * TOP_K
    xt = x.reshape(N, D)
    logits = (xt @ w_router + b_router).astype(jnp.float32)
    top_vals, top_idx = lax.top_k(logits, TOP_K)
    gates = jax.nn.softmax(top_vals, axis=-1)
    e_flat = top_idx.reshape(A)
    tok_flat = jnp.repeat(jnp.arange(N, dtype=jnp.int32), TOP_K)
    w_flat = gates.reshape(A)
    order = jnp.argsort(e_flat)
    e_sorted = e_flat[order]
    counts = jnp.bincount(e_flat, length=N_EXPERTS)
    padded = ((counts + MOE_BLOCK - 1) // MOE_BLOCK) * MOE_BLOCK
    start = jnp.cumsum(counts) - counts
    pend = jnp.cumsum(padded)
    pstart = pend - padded
    dest = pstart[e_sorted] + (jnp.arange(A) - start[e_sorted])
    R = A + N_EXPERTS * MOE_BLOCK
    nblk = R // MOE_BLOCK
    row_tok = jnp.zeros((R,), jnp.int32).at[dest].set(tok_flat[order])
    row_w = jnp.zeros((R,), jnp.float32).at[dest].set(w_flat[order])
    blk_expert = jnp.minimum(
        jnp.searchsorted(pend, jnp.arange(nblk) * MOE_BLOCK, side='right'), N_EXPERTS - 1)
    xs = xt[row_tok].reshape(nblk, MOE_BLOCK, D)

    def expert_block(args):
        xb, e = args
        h = xb @ w_mlp1[e] + b_mlp1[e]
        gate = jnp.minimum(h[:, :D_EXPERT], SWIGLU_LIMIT)
        up = jnp.clip(h[:, D_EXPERT:], -SWIGLU_LIMIT, SWIGLU_LIMIT)
        act = (up + 1.0) * gate * jax.nn.sigmoid(SWIGLU_ALPHA * gate)
        return act @ w_mlp2[e] + b_mlp2[e]

    ys = lax.map(expert_block, (xs, blk_expert)).reshape(R, D)
    ys = ys * row_w[:, None].astype(ys.dtype)
    out = jax.ops.segment_sum(ys, row_tok, num_segments=N)
    return out.reshape(B, S, D)


def setup_inputs(seed: int = 0) -> dict:
    key = jax.random.key(seed)
    ks = jax.random.split(key, 20)
    L, D, E, F = DEPTH, D_MODEL, N_EXPERTS, D_EXPERT
    nrm = lambda k, shape: jax.random.normal(k, shape, jnp.float32)
    col_scale = np.ones((IN_WIDTH,), np.float32)
    v0 = ATTN_WIDTH + KV_WIDTH
    col_scale[v0:v0 + KV_WIDTH] = DEEPNORM_BETA
    w_in = nrm(ks[1], (L, D, IN_WIDTH)) * (D ** -0.5) * jnp.asarray(col_scale)
    return {
        'x': nrm(ks[0], (BATCH, SEQ, D)),
        'w_in': w_in,
        'attn_sinks': nrm(ks[2], (L, N_Q_HEADS)),
        'w_attn_branch': nrm(ks[3], (L, ATTN_WIDTH, D)) * (ATTN_WIDTH ** -0.5),
        'w_pool_group': nrm(ks[4], (L, N_POOL_GROUPS, POOL_GROUP_WIDTH, POOL_GROUP_WIDTH)) * (POOL_GROUP_WIDTH ** -0.5),
        'pool_scale': 1.0 + 0.1 * nrm(ks[5], (L, POOL_WIDTH)),
        'w_pool_branch': nrm(ks[6], (L, POOL_WIDTH, D)) * (POOL_WIDTH ** -0.5),
        'w_out': nrm(ks[7], (L, D, D)) * (D ** -0.5) * DEEPNORM_BETA,
        'ln1_g': 1.0 + 0.02 * nrm(ks[8], (L, D)),
        'ln1_b': 0.02 * nrm(ks[9], (L, D)),
        'w_router': nrm(ks[10], (L, D, E)) * (D ** -0.5),
        'b_router': 0.01 * nrm(ks[11], (L, E)),
        'w_mlp1': nrm(ks[12], (L, E, D, 2 * F)) * (D ** -0.5) * DEEPNORM_BETA,
        'b_mlp1': 0.01 * nrm(ks[13], (L, E, 2 * F)),
        'w_mlp2': nrm(ks[14], (L, E, F, D)) * (F ** -0.5) * DEEPNORM_BETA,
        'b_mlp2': 0.01 * nrm(ks[15], (L, E, D)),
        'ln2_g': 1.0 + 0.02 * nrm(ks[16], (L, D)),
        'ln2_b': 0.02 * nrm(ks[17], (L, D)),
    }


def reference(x, w_in, attn_sinks, w_attn_branch, w_pool_group, pool_scale, w_pool_branch,
              w_out, ln1_g, ln1_b, w_router, b_router, w_mlp1, b_mlp1, w_mlp2, b_mlp2,
              ln2_g, ln2_b):
    B, S, D = x.shape
    o_k = ATTN_WIDTH
    o_v = o_k + KV_WIDTH
    o_p = o_v + KV_WIDTH
    o_g = o_p + POOL_WIDTH
    h = x
    for l in range(DEPTH):
        u = h @ w_in[l]
        q = u[..., :o_k].reshape(B, S, N_Q_HEADS, HEAD_DIM)
        k = u[..., o_k:o_v].reshape(B, S, N_KV_HEADS, HEAD_DIM)
        v = u[..., o_v:o_p].reshape(B, S, N_KV_HEADS, HEAD_DIM)
        p_in = u[..., o_p:o_g]
        g_attn = jax.nn.sigmoid(u[..., o_g:o_g + D])
        g_pool = jax.nn.sigmoid(u[..., o_g + D:o_g + 2 * D])
        a_br = windowed_attention(q, k, v, attn_sinks[l]) @ w_attn_branch[l]
        p_br = multiscale_pool(p_in, w_pool_group[l], pool_scale[l]) @ w_pool_branch[l]
        mixed = g_attn * a_br + g_pool * p_br
        h = layer_norm(DEEPNORM_ALPHA * h + mixed @ w_out[l], ln1_g[l], ln1_b[l])
        f = moe_ffn(h, w_router[l], b_router[l], w_mlp1[l], b_mlp1[l], w_mlp2[l], b_mlp2[l])
        h = layer_norm(DEEPNORM_ALPHA * h + f, ln2_g[l], ln2_b[l])
    return h
```

```python
import functools
import math

import jax
import jax.numpy as jnp
import numpy as np
from jax import lax
from jax.experimental import pallas as pl
from jax.experimental.pallas import tpu as pltpu

F32 = jnp.float32
BF16 = jnp.bfloat16

N_Q_HEADS = 8
N_KV_HEADS = 2
HEAD_DIM = 64
GROUP = N_Q_HEADS // N_KV_HEADS
ATTN_WIDTH = N_Q_HEADS * HEAD_DIM
KV_WIDTH = N_KV_HEADS * HEAD_DIM
WINDOW = 128
BLOCK = 128
POOL_WINDOWS = (2, 4, 8, 16)
POOL_GROUP_WIDTH = 128
POOL_WIDTH = len(POOL_WINDOWS) * POOL_GROUP_WIDTH
POOL_HALO = 8
N_EXPERTS = 32
TOP_K = 4
SWIGLU_LIMIT = 7.0
SWIGLU_ALPHA = 1.702
LN_EPS = 1e-5
DEPTH = 1
DEEPNORM_ALPHA = (2.0 * DEPTH) ** 0.25
NEG_BIG = -1e30

TM_PROJ = 512
TM_MERGE = 256
TM_ROWS = 256
BM_FFN = 256
VMEM_LIMIT = 56 * 1024 * 1024


def _sigmoid(x):
    return 1.0 / (1.0 + jnp.exp(-x))


def _layer_norm(x, g, b):
    mean = jnp.mean(x, axis=-1, keepdims=True)
    xc = x - mean
    var = jnp.mean(xc * xc, axis=-1, keepdims=True)
    return xc * lax.rsqrt(var + LN_EPS) * g + b


def _inproj_kernel(x_ref, w_ref, q_ref, k_ref, v_ref, p_ref, ga_ref, gp_ref, *, d_model):
    xb = x_ref[...].astype(BF16)
    o_k = ATTN_WIDTH
    o_v = o_k + KV_WIDTH
    o_p = o_v + KV_WIDTH
    o_g = o_p + POOL_WIDTH

    def proj(lo, hi):
        return jnp.dot(xb, w_ref[:, lo:hi], preferred_element_type=F32)

    q_ref[...] = (proj(0, o_k) * (1.0 / math.sqrt(HEAD_DIM))).astype(BF16)
    k_ref[...] = proj(o_k, o_v).astype(BF16)
    v_ref[...] = proj(o_v, o_p).astype(BF16)
    p_ref[...] = proj(o_p, o_g)
    ga_ref[...] = _sigmoid(proj(o_g, o_g + d_model))
    gp_ref[...] = _sigmoid(proj(o_g + d_model, o_g + 2 * d_model))


def _inproj(x2, w_in_b):
    n, d = x2.shape
    in_width = w_in_b.shape[1]
    tm = TM_PROJ
    row = lambda i: (i, 0)
    return pl.pallas_call(
        functools.partial(_inproj_kernel, d_model=d),
        grid=(n // tm,),
        in_specs=[pl.BlockSpec((tm, d), row),
                  pl.BlockSpec((d, in_width), lambda i: (0, 0))],
        out_specs=[pl.BlockSpec((tm, ATTN_WIDTH), row),
                   pl.BlockSpec((tm, KV_WIDTH), row),
                   pl.BlockSpec((tm, KV_WIDTH), row),
                   pl.BlockSpec((tm, POOL_WIDTH), row),
                   pl.BlockSpec((tm, d), row),
                   pl.BlockSpec((tm, d), row)],
        out_shape=[jax.ShapeDtypeStruct((n, ATTN_WIDTH), BF16),
                   jax.ShapeDtypeStruct((n, KV_WIDTH), BF16),
                   jax.ShapeDtypeStruct((n, KV_WIDTH), BF16),
                   jax.ShapeDtypeStruct((n, POOL_WIDTH), F32),
                   jax.ShapeDtypeStruct((n, d), F32),
                   jax.ShapeDtypeStruct((n, d), F32)],
        compiler_params=pltpu.CompilerParams(
            dimension_semantics=("arbitrary",), vmem_limit_bytes=VMEM_LIMIT),
        name="inproj",
    )(x2, w_in_b)


def _alibi_slope(h):
    return 2.0 ** (-8.0 * (h + 1) / N_Q_HEADS)


def _attn_kernel(sink_ref, q_ref, kp_ref, kc_ref, kn_ref, vp_ref, vc_ref, vn_ref, o_ref, *, nb):
    j = pl.program_id(0) % nb
    kk = jnp.concatenate([kp_ref[...], kc_ref[...], kn_ref[...]], axis=0)
    vv = jnp.concatenate([vp_ref[...], vc_ref[...], vn_ref[...]], axis=0)
    qi = lax.broadcasted_iota(jnp.int32, (BLOCK, 3 * BLOCK), 0)
    ki = lax.broadcasted_iota(jnp.int32, (BLOCK, 3 * BLOCK), 1)
    rel = ki - BLOCK - qi
    arel = jnp.abs(rel)
    valid = arel <= WINDOW
    valid = valid & ((ki >= BLOCK) | (j > 0)) & ((ki < 2 * BLOCK) | (j < nb - 1))
    arel_f = arel.astype(F32)
    for h in range(N_Q_HEADS):
        kvh = h // GROUP
        qh = q_ref[:, h * HEAD_DIM:(h + 1) * HEAD_DIM]
        kh = kk[:, kvh * HEAD_DIM:(kvh + 1) * HEAD_DIM]
        vh = vv[:, kvh * HEAD_DIM:(kvh + 1) * HEAD_DIM]
        s = lax.dot_general(qh, kh, (((1,), (1,)), ((), ())), preferred_element_type=F32)
        s = jnp.where(valid, s - _alibi_slope(h) * arel_f, NEG_BIG)
        sink = sink_ref[h]
        m = jnp.maximum(jnp.max(s, axis=-1, keepdims=True), sink)
        p = jnp.exp(s - m)
        denom = jnp.sum(p, axis=-1, keepdims=True) + jnp.exp(sink - m)
        o = jnp.dot(p.astype(BF16), vh, preferred_element_type=F32) / denom
        o_ref[:, h * HEAD_DIM:(h + 1) * HEAD_DIM] = o.astype(BF16)


def _attention(q, k, v, sinks, seq):
    n = q.shape[0]
    nb = seq // BLOCK
    nblk = n // BLOCK

    def prev_map(i, s):
        return (jnp.where(i % nb == 0, i, i - 1), 0)

    def next_map(i, s):
        return (jnp.where(i % nb == nb - 1, i, i + 1), 0)

    cur_map = lambda i, s: (i, 0)
    kv_spec = lambda m: pl.BlockSpec((BLOCK, KV_WIDTH), m)
    return pl.pallas_call(
        functools.partial(_attn_kernel, nb=nb),
        grid_spec=pltpu.PrefetchScalarGridSpec(
            num_scalar_prefetch=1,
            grid=(nblk,),
            in_specs=[pl.BlockSpec((BLOCK, ATTN_WIDTH), cur_map),
                      kv_spec(prev_map), kv_spec(cur_map), kv_spec(next_map),
                      kv_spec(prev_map), kv_spec(cur_map), kv_spec(next_map)],
            out_specs=pl.BlockSpec((BLOCK, ATTN_WIDTH), cur_map)),
        out_shape=jax.ShapeDtypeStruct((n, ATTN_WIDTH), BF16),
        compiler_params=pltpu.CompilerParams(
            dimension_semantics=("arbitrary",), vmem_limit_bytes=VMEM_LIMIT),
        name="attn",
    )(sinks, q, k, k, k, v, v, v)


def _merge_kernel(x_ref, a_ref, pc_ref, pp_ref, pn_ref, ga_ref, gp_ref,
                  wab_ref, wpg_ref, ps_ref, wpb_ref, wo_ref, g1_ref, b1_ref, wr_ref, br_ref,
                  h_ref, idx_ref, gate_ref, rank_ref, cnt_ref,
                  ext_ref, carry_ref, *, seq, tm):
    i = pl.program_id(0)
    tiles_per_seq = seq // tm
    jt = i % tiles_per_seq

    @pl.when(i == 0)
    def _():
        carry_ref[...] = jnp.zeros_like(carry_ref)

    ext_ref[0:POOL_HALO, :] = jnp.where(jt > 0, pp_ref[...], 0.0)
    ext_ref[POOL_HALO:POOL_HALO + tm, :] = pc_ref[...]
    ext_ref[POOL_HALO + tm:, :] = jnp.where(jt < tiles_per_seq - 1, pn_ref[...], 0.0)
    spos = jt * tm + lax.broadcasted_iota(jnp.int32, (tm, 1), 0)
    pooled_parts = []
    for g, w in enumerate(POOL_WINDOWS):
        cs = slice(g * POOL_GROUP_WIDTH, (g + 1) * POOL_GROUP_WIDTH)
        acc = None
        for off in range(-(w // 2), w // 2):
            t = ext_ref[POOL_HALO + off:POOL_HALO + off + tm, cs]
            acc = t if acc is None else acc + t
        hi = jnp.minimum(spos + (w // 2 - 1), seq - 1)
        lo = jnp.maximum(spos - w // 2, 0)
        cnt = (hi - lo + 1).astype(F32)
        y = acc / cnt - pc_ref[:, cs]
        yg = jnp.dot(y.astype(BF16), wpg_ref[g], preferred_element_type=F32)
        pooled_parts.append(yg * ps_ref[:, cs])
    pool_feat = jnp.concatenate(pooled_parts, axis=-1).astype(BF16)

    a_br = jnp.dot(a_ref[...], wab_ref[...], preferred_element_type=F32)
    p_br = jnp.dot(pool_feat, wpb_ref[...], preferred_element_type=F32)
    mixed = ga_ref[...] * a_br + gp_ref[...] * p_br
    o = jnp.dot(mixed.astype(BF16), wo_ref[...], preferred_element_type=F32)
    h = _layer_norm(DEEPNORM_ALPHA * x_ref[...] + o, g1_ref[...], b1_ref[...])
    h_ref[...] = h

    logits = jnp.dot(h, wr_ref[...], preferred_element_type=F32,
                     precision=lax.Precision.HIGHEST) + br_ref[...]
    eidx = lax.broadcasted_iota(jnp.int32, (tm, N_EXPERTS), 1)
    work = logits
    sel_idx, sel_val = [], []
    chosen = jnp.zeros((tm, N_EXPERTS), F32)
    for _ in range(TOP_K):
        m = jnp.max(work, axis=-1, keepdims=True)
        ix = jnp.min(jnp.where(work == m, eidx, N_EXPERTS), axis=-1, keepdims=True)
        hit = eidx == ix
        sel_idx.append(ix)
        sel_val.append(m)
        chosen = jnp.where(hit, 1.0, chosen)
        work = jnp.where(hit, -jnp.inf, work)
    ex = [jnp.exp(v - sel_val[0]) for v in sel_val]
    tot = ex[0] + ex[1] + ex[2] + ex[3]
    gate_ref[...] = jnp.concatenate([e / tot for e in ex], axis=-1)
    idx_ref[...] = jnp.concatenate(sel_idx, axis=-1)

    r_i = lax.broadcasted_iota(jnp.int32, (tm, tm), 0)
    c_i = lax.broadcasted_iota(jnp.int32, (tm, tm), 1)
    lower = (c_i < r_i).astype(BF16)
    before = jnp.dot(lower, chosen.astype(BF16), preferred_element_type=F32) + carry_ref[...]
    ranks = [jnp.sum(jnp.where(eidx == ix, before, 0.0), axis=-1, keepdims=True) for ix in sel_idx]
    rank_ref[...] = jnp.concatenate(ranks, axis=-1).astype(jnp.int32)
    carry_ref[...] = carry_ref[...] + jnp.sum(chosen, axis=0, keepdims=True)
    cnt_ref[...] = carry_ref[...].astype(jnp.int32)


def _merge(x2, attn, p_in, g_attn, g_pool, wab, wpg, ps, wpb, wo, g1, b1, wr, br, seq):
    n, d = x2.shape
    tm = TM_MERGE
    halo_blocks = tm // POOL_HALO
    row = lambda i: (i, 0)
    full2 = lambda i: (0, 0)

    def prev_halo(i):
        return (jnp.maximum(i * halo_blocks - 1, 0), 0)

    def next_halo(i):
        return (jnp.minimum((i + 1) * halo_blocks, n // POOL_HALO - 1), 0)

    return pl.pallas_call(
        functools.partial(_merge_kernel, seq=seq, tm=tm),
        grid=(n // tm,),
        in_specs=[pl.BlockSpec((tm, d), row),
                  pl.BlockSpec((tm, ATTN_WIDTH), row),
                  pl.BlockSpec((tm, POOL_WIDTH), row),
                  pl.BlockSpec((POOL_HALO, POOL_WIDTH), prev_halo),
                  pl.BlockSpec((POOL_HALO, POOL_WIDTH), next_halo),
                  pl.BlockSpec((tm, d), row),
                  pl.BlockSpec((tm, d), row),
                  pl.BlockSpec(wab.shape, full2),
                  pl.BlockSpec(wpg.shape, lambda i: (0, 0, 0)),
                  pl.BlockSpec(ps.shape, full2),
                  pl.BlockSpec(wpb.shape, full2),
                  pl.BlockSpec(wo.shape, full2),
                  pl.BlockSpec(g1.shape, full2),
                  pl.BlockSpec(b1.shape, full2),
                  pl.BlockSpec(wr.shape, full2),
                  pl.BlockSpec(br.shape, full2)],
        out_specs=[pl.BlockSpec((tm, d), row),
                   pl.BlockSpec((tm, TOP_K), row),
                   pl.BlockSpec((tm, TOP_K), row),
                   pl.BlockSpec((tm, TOP_K), row),
                   pl.BlockSpec((1, N_EXPERTS), full2)],
        out_shape=[jax.ShapeDtypeStruct((n, d), F32),
                   jax.ShapeDtypeStruct((n, TOP_K), jnp.int32),
                   jax.ShapeDtypeStruct((n, TOP_K), F32),
                   jax.ShapeDtypeStruct((n, TOP_K), jnp.int32),
                   jax.ShapeDtypeStruct((1, N_EXPERTS), jnp.int32)],
        scratch_shapes=[pltpu.VMEM((tm + 2 * POOL_HALO, POOL_WIDTH), F32),
                        pltpu.VMEM((1, N_EXPERTS), F32)],
        compiler_params=pltpu.CompilerParams(
            dimension_semantics=("arbitrary",), vmem_limit_bytes=VMEM_LIMIT),
        name="merge",
    )(x2, attn, p_in, p_in, p_in, g_attn, g_pool, wab, wpg, ps, wpb, wo, g1, b1, wr, br)


def _dispatch_kernel(pos_ref, pend_ref, nu_ref, h_ref, xs_ref, zero_ref, sem, zsem, *, tm, bm, nblk):
    base = pl.program_id(0) * (tm * TOP_K)

    @pl.when(pl.program_id(0) == 0)
    def _():
        zero_ref[...] = jnp.zeros_like(zero_ref)

        def fill(start):
            return pltpu.make_async_copy(zero_ref, xs_ref.at[pl.ds(start, bm), :], zsem)

        def group_tail(e):
            prev_end = jnp.where(e > 0, pend_ref[jnp.maximum(e - 1, 0)], 0)
            return pend_ref[e] > prev_end, pl.multiple_of(pend_ref[e] - bm, bm)

        def per_expert(op):
            def body(e, c):
                nonempty, start = group_tail(e)

                @pl.when(nonempty)
                def _():
                    op(fill(start))
                return c
            lax.fori_loop(0, N_EXPERTS, body, 0)

        def per_tail_block(op):
            def body(b, c):
                op(fill(pl.multiple_of(b * bm, bm)))
                return c
            lax.fori_loop(nu_ref[0], nblk, body, 0)

        per_expert(lambda c: c.start())
        per_tail_block(lambda c: c.start())
        per_expert(lambda c: c.wait())
        per_tail_block(lambda c: c.wait())

    def row_copy(t, k):
        dst = pos_ref[base + t * TOP_K + k]
        return pltpu.make_async_copy(h_ref.at[pl.ds(t, 1), :], xs_ref.at[pl.ds(dst, 1), :], sem)

    def issue(t, c):
        for k in range(TOP_K):
            row_copy(t, k).start()
        return c

    lax.fori_loop(0, tm, issue, 0)

    def drain(t, c):
        for k in range(TOP_K):
            row_copy(t, k).wait()
        return c

    lax.fori_loop(0, tm, drain, 0)


def _dispatch(pos_flat, pend, n_used, h, n_rows):
    n, d = h.shape
    tm = TM_ROWS
    bm = BM_FFN
    return pl.pallas_call(
        functools.partial(_dispatch_kernel, tm=tm, bm=bm, nblk=n_rows // bm),
        grid_spec=pltpu.PrefetchScalarGridSpec(
            num_scalar_prefetch=3,
            grid=(n // tm,),
            in_specs=[pl.BlockSpec((tm, d), lambda i, p, e, u: (i, 0))],
            out_specs=pl.BlockSpec(memory_space=pl.ANY),
            scratch_shapes=[pltpu.VMEM((bm, d), F32),
                            pltpu.SemaphoreType.DMA,
                            pltpu.SemaphoreType.DMA]),
        out_shape=jax.ShapeDtypeStruct((n_rows, d), F32),
        compiler_params=pltpu.CompilerParams(
            dimension_semantics=("arbitrary",), vmem_limit_bytes=VMEM_LIMIT),
        name="dispatch",
    )(pos_flat, pend, n_used, h)


def _ffn_kernel(be_ref, nv_ref, nu_ref, x_ref, w1_ref, b1_ref, w2_ref, b2_ref, y_ref, *, bm, f):
    b = pl.program_id(0)

    @pl.when(b < nu_ref[0])
    def _():
        rows = lax.broadcasted_iota(jnp.int32, (bm, 1), 0)
        x = jnp.where(rows < nv_ref[b], x_ref[...], 0.0).astype(BF16)
        h = jnp.dot(x, w1_ref[0], preferred_element_type=F32) + b1_ref[0]
        gate = jnp.minimum(h[:, :f], SWIGLU_LIMIT)
        up = jnp.clip(h[:, f:], -SWIGLU_LIMIT, SWIGLU_LIMIT)
        act = (up + 1.0) * gate * _sigmoid(SWIGLU_ALPHA * gate)
        y_ref[...] = jnp.dot(act.astype(BF16), w2_ref[0], preferred_element_type=F32) + b2_ref[0]

    @pl.when(b >= nu_ref[0])
    def _():
        y_ref[...] = jnp.zeros_like(y_ref)


def _ffn(blk_expert, blk_valid, n_used, xs, w1, b1, w2, b2):
    n_rows, d = xs.shape
    bm = BM_FFN
    f = w2.shape[1]

    def row_map(b, be, nv, nu):
        return (jnp.minimum(b, nu[0] - 1), 0)

    def exp_map(b, be, nv, nu):
        return (be[b], 0, 0)

    return pl.pallas_call(
        functools.partial(_ffn_kernel, bm=bm, f=f),
        grid_spec=pltpu.PrefetchScalarGridSpec(
            num_scalar_prefetch=3,
            grid=(n_rows // bm,),
            in_specs=[pl.BlockSpec((bm, d), row_map),
                      pl.BlockSpec((1, d, 2 * f), exp_map),
                      pl.BlockSpec((1, 1, 2 * f), exp_map),
                      pl.BlockSpec((1, f, d), exp_map),
                      pl.BlockSpec((1, 1, d), exp_map)],
            out_specs=pl.BlockSpec((bm, d), lambda b, be, nv, nu: (b, 0))),
        out_shape=jax.ShapeDtypeStruct((n_rows, d), F32),
        compiler_params=pltpu.CompilerParams(
            dimension_semantics=("arbitrary",), vmem_limit_bytes=VMEM_LIMIT),
        name="ffn",
    )(blk_expert, blk_valid, n_used, xs, w1, b1, w2, b2)


def _combine_kernel(pos_ref, h_ref, gate_ref, g2_ref, b2_ref, ys_ref, o_ref, buf_ref, sem, *, tm):
    base = pl.program_id(0) * (tm * TOP_K)

    def row_copy(t, k):
        src = pos_ref[base + t * TOP_K + k]
        return pltpu.make_async_copy(ys_ref.at[pl.ds(src, 1), :], buf_ref.at[k, pl.ds(t, 1), :], sem)

    def issue(t, c):
        for k in range(TOP_K):
            row_copy(t, k).start()
        return c

    lax.fori_loop(0, tm, issue, 0)

    def drain(t, c):
        for k in range(TOP_K):
            row_copy(t, k).wait()
        return c

    lax.fori_loop(0, tm, drain, 0)

    f = gate_ref[:, 0:1] * buf_ref[0]
    for k in range(1, TOP_K):
        f = f + gate_ref[:, k:k + 1] * buf_ref[k]
    o_ref[...] = _layer_norm(DEEPNORM_ALPHA * h_ref[...] + f, g2_ref[...], b2_ref[...])


def _combine(pos_flat, h, gates, g2, b2, ys):
    n, d = h.shape
    tm = TM_ROWS
    row = lambda i, p: (i, 0)
    full2 = lambda i, p: (0, 0)
    return pl.pallas_call(
        functools.partial(_combine_kernel, tm=tm),
        grid_spec=pltpu.PrefetchScalarGridSpec(
            num_scalar_prefetch=1,
            grid=(n // tm,),
            in_specs=[pl.BlockSpec((tm, d), row),
                      pl.BlockSpec((tm, TOP_K), row),
                      pl.BlockSpec(g2.shape, full2),
                      pl.BlockSpec(b2.shape, full2),
                      pl.BlockSpec(memory_space=pl.ANY)],
            out_specs=pl.BlockSpec((tm, d), row),
            scratch_shapes=[pltpu.VMEM((TOP_K, tm, d), F32),
                            pltpu.SemaphoreType.DMA]),
        out_shape=jax.ShapeDtypeStruct((n, d), F32),
        compiler_params=pltpu.CompilerParams(
            dimension_semantics=("arbitrary",), vmem_limit_bytes=VMEM_LIMIT),
        name="combine",
    )(pos_flat, h, gates, g2, b2, ys)


def _layer(h2, seq, w_in, sinks, wab, wpg, ps, wpb, wo, g1, b1, wr, br, w1, bb1, w2, bb2, g2, b2):
    n, d = h2.shape
    q, k, v, p_in, g_attn, g_pool = _inproj(h2, w_in.astype(BF16))
    attn = _attention(q, k, v, sinks, seq)
    h1, idx, gates, rank, counts = _merge(
        h2, attn, p_in, g_attn, g_pool,
        wab.astype(BF16), wpg.astype(BF16), ps.reshape(1, -1), wpb.astype(BF16), wo.astype(BF16),
        g1.reshape(1, -1), b1.reshape(1, -1), wr, br.reshape(1, -1), seq)

    bm = BM_FFN
    counts = counts.reshape(N_EXPERTS)
    padded = ((counts + bm - 1) // bm) * bm
    pend = jnp.cumsum(padded)
    pstart = pend - padded
    n_rows = n * TOP_K + N_EXPERTS * bm
    nblk = n_rows // bm
    blk_lo = jnp.arange(nblk, dtype=jnp.int32) * bm
    blk_expert = jnp.minimum(
        jnp.sum((blk_lo[:, None] >= pend[None, :]).astype(jnp.int32), axis=1), N_EXPERTS - 1)
    blk_valid = jnp.clip((pstart + counts)[blk_expert] - blk_lo, 0, bm).astype(jnp.int32)
    n_used = (pend[-1:] // bm).astype(jnp.int32)
    onehot = idx[:, :, None] == jnp.arange(N_EXPERTS, dtype=jnp.int32)[None, None, :]
    pos = rank + jnp.sum(jnp.where(onehot, pstart[None, None, :], 0), axis=-1)
    pos_flat = pos.reshape(n * TOP_K).astype(jnp.int32)

    xs = _dispatch(pos_flat, pend.astype(jnp.int32), n_used, h1, n_rows)
    ys = _ffn(blk_expert.astype(jnp.int32), blk_valid, n_used, xs,
              w1.astype(BF16), bb1.reshape(N_EXPERTS, 1, -1), w2.astype(BF16),
              bb2.reshape(N_EXPERTS, 1, -1))
    return _combine(pos_flat, h1, gates, g2.reshape(1, -1), b2.reshape(1, -1), ys)


def kernel(x, w_in, attn_sinks, w_attn_branch, w_pool_group, pool_scale, w_pool_branch, w_out,
           ln1_g, ln1_b, w_router, b_router, w_mlp1, b_mlp1, w_mlp2, b_mlp2, ln2_g, ln2_b):
    bsz, seq, d = x.shape
    h = x.reshape(bsz * seq, d)
    for l in range(w_in.shape[0]):
        h = _layer(h, seq, w_in[l], attn_sinks[l], w_attn_branch[l], w_pool_group[l], pool_scale[l],
                   w_pool_branch[l], w_out[l], ln1_g[l], ln1_b[l], w_router[l], b_router[l],
                   w_mlp1[l], b_mlp1[l], w_mlp2[l], b_mlp2[l], ln2_g[l], ln2_b[l])
    return h.reshape(bsz, seq, d)
```

```python
import functools
import math

import jax
import jax.numpy as jnp
import numpy as np
from jax import lax
from jax.experimental import pallas as pl
from jax.experimental.pallas import tpu as pltpu

F32 = jnp.float32
BF16 = jnp.bfloat16

N_Q_HEADS = 8
N_KV_HEADS = 2
HEAD_DIM = 64
GROUP = N_Q_HEADS // N_KV_HEADS
ATTN_WIDTH = N_Q_HEADS * HEAD_DIM
KV_WIDTH = N_KV_HEADS * HEAD_DIM
WINDOW = 128
BLOCK = 128
POOL_WINDOWS = (2, 4, 8, 16)
POOL_GROUP_WIDTH = 128
POOL_WIDTH = len(POOL_WINDOWS) * POOL_GROUP_WIDTH
POOL_HALO = 8
N_EXPERTS = 32
TOP_K = 4
SWIGLU_LIMIT = 7.0
SWIGLU_ALPHA = 1.702
LN_EPS = 1e-5
DEPTH = 1
DEEPNORM_ALPHA = (2.0 * DEPTH) ** 0.25
NEG_BIG = -1e30

TM_PROJ = 512
TM_MERGE = 256
TM_ROWS = 512
ROW_UNROLL = 2
BM_FFN = 512
VMEM_LIMIT = 56 * 1024 * 1024


def _sigmoid(x):
    return 1.0 / (1.0 + jnp.exp(-x))


def _layer_norm(x, g, b):
    mean = jnp.mean(x, axis=-1, keepdims=True)
    xc = x - mean
    var = jnp.mean(xc * xc, axis=-1, keepdims=True)
    return xc * lax.rsqrt(var + LN_EPS) * g + b


def _inproj_kernel(x_ref, w_ref, q_ref, k_ref, v_ref, p_ref, ga_ref, gp_ref, *, d_model):
    xb = x_ref[...].astype(BF16)
    o_k = ATTN_WIDTH
    o_v = o_k + KV_WIDTH
    o_p = o_v + KV_WIDTH
    o_g = o_p + POOL_WIDTH

    def proj(lo, hi):
        return jnp.dot(xb, w_ref[:, lo:hi], preferred_element_type=F32)

    q_ref[...] = (proj(0, o_k) * (1.0 / math.sqrt(HEAD_DIM))).astype(BF16)
    k_ref[...] = proj(o_k, o_v).astype(BF16)
    v_ref[...] = proj(o_v, o_p).astype(BF16)
    p_ref[...] = proj(o_p, o_g)
    ga_ref[...] = _sigmoid(proj(o_g, o_g + d_model))
    gp_ref[...] = _sigmoid(proj(o_g + d_model, o_g + 2 * d_model))


def _inproj(x2, w_in_b):
    n, d = x2.shape
    in_width = w_in_b.shape[1]
    tm = TM_PROJ
    row = lambda i: (i, 0)
    return pl.pallas_call(
        functools.partial(_inproj_kernel, d_model=d),
        grid=(n // tm,),
        in_specs=[pl.BlockSpec((tm, d), row),
                  pl.BlockSpec((d, in_width), lambda i: (0, 0))],
        out_specs=[pl.BlockSpec((tm, ATTN_WIDTH), row),
                   pl.BlockSpec((tm, KV_WIDTH), row),
                   pl.BlockSpec((tm, KV_WIDTH), row),
                   pl.BlockSpec((tm, POOL_WIDTH), row),
                   pl.BlockSpec((tm, d), row),
                   pl.BlockSpec((tm, d), row)],
        out_shape=[jax.ShapeDtypeStruct((n, ATTN_WIDTH), BF16),
                   jax.ShapeDtypeStruct((n, KV_WIDTH), BF16),
                   jax.ShapeDtypeStruct((n, KV_WIDTH), BF16),
                   jax.ShapeDtypeStruct((n, POOL_WIDTH), F32),
                   jax.ShapeDtypeStruct((n, d), F32),
                   jax.ShapeDtypeStruct((n, d), F32)],
        compiler_params=pltpu.CompilerParams(
            dimension_semantics=("arbitrary",), vmem_limit_bytes=VMEM_LIMIT),
        name="inproj",
    )(x2, w_in_b)


def _alibi_slope(h):
    return 2.0 ** (-8.0 * (h + 1) / N_Q_HEADS)


def _attn_kernel(sink_ref, q_ref, kp_ref, kc_ref, kn_ref, vp_ref, vc_ref, vn_ref, o_ref, *, nb):
    j = pl.program_id(0) % nb
    kk = jnp.concatenate([kp_ref[...], kc_ref[...], kn_ref[...]], axis=0)
    vv = jnp.concatenate([vp_ref[...], vc_ref[...], vn_ref[...]], axis=0)
    rows = GROUP * BLOCK
    ri = lax.broadcasted_iota(jnp.int32, (rows, 3 * BLOCK), 0)
    ki = lax.broadcasted_iota(jnp.int32, (rows, 3 * BLOCK), 1)
    arel = jnp.abs(ki - BLOCK - ri % BLOCK)
    valid = arel <= WINDOW
    valid = valid & ((ki >= BLOCK) | (j > 0)) & ((ki < 2 * BLOCK) | (j < nb - 1))
    arel_f = arel.astype(F32)
    grp = lax.broadcasted_iota(jnp.int32, (rows, 1), 0) // BLOCK

    def per_group_column(values):
        col = jnp.full((rows, 1), values[GROUP - 1], F32)
        for g in range(GROUP - 2, -1, -1):
            col = jnp.where(grp == g, values[g], col)
        return col

    for kvh in range(N_KV_HEADS):
        heads = [kvh * GROUP + g for g in range(GROUP)]
        slope = per_group_column([_alibi_slope(h) for h in heads])
        sink = per_group_column([sink_ref[h] for h in heads])
        qs = jnp.concatenate([q_ref[:, h * HEAD_DIM:(h + 1) * HEAD_DIM] for h in heads], axis=0)
        kh = kk[:, kvh * HEAD_DIM:(kvh + 1) * HEAD_DIM]
        vh = vv[:, kvh * HEAD_DIM:(kvh + 1) * HEAD_DIM]
        s = lax.dot_general(qs, kh, (((1,), (1,)), ((), ())), preferred_element_type=F32)
        s = jnp.where(valid, s - slope * arel_f, NEG_BIG)
        m = jnp.maximum(jnp.max(s, axis=-1, keepdims=True), sink)
        p = jnp.exp(s - m)
        denom = jnp.sum(p, axis=-1, keepdims=True) + jnp.exp(sink - m)
        o = (jnp.dot(p.astype(BF16), vh, preferred_element_type=F32) / denom).astype(BF16)
        for g, h in enumerate(heads):
            o_ref[:, h * HEAD_DIM:(h + 1) * HEAD_DIM] = o[g * BLOCK:(g + 1) * BLOCK]


def _attention(q, k, v, sinks, seq):
    n = q.shape[0]
    nb = seq // BLOCK
    nblk = n // BLOCK

    def prev_map(i, s):
        return (jnp.where(i % nb == 0, i, i - 1), 0)

    def next_map(i, s):
        return (jnp.where(i % nb == nb - 1, i, i + 1), 0)

    cur_map = lambda i, s: (i, 0)
    kv_spec = lambda m: pl.BlockSpec((BLOCK, KV_WIDTH), m)
    return pl.pallas_call(
        functools.partial(_attn_kernel, nb=nb),
        grid_spec=pltpu.PrefetchScalarGridSpec(
            num_scalar_prefetch=1,
            grid=(nblk,),
            in_specs=[pl.BlockSpec((BLOCK, ATTN_WIDTH), cur_map),
                      kv_spec(prev_map), kv_spec(cur_map), kv_spec(next_map),
                      kv_spec(prev_map), kv_spec(cur_map), kv_spec(next_map)],
            out_specs=pl.BlockSpec((BLOCK, ATTN_WIDTH), cur_map)),
        out_shape=jax.ShapeDtypeStruct((n, ATTN_WIDTH), BF16),
        compiler_params=pltpu.CompilerParams(
            dimension_semantics=("arbitrary",), vmem_limit_bytes=VMEM_LIMIT),
        name="attn",
    )(sinks, q, k, k, k, v, v, v)


def _merge_kernel(x_ref, a_ref, pc_ref, pp_ref, pn_ref, ga_ref, gp_ref,
                  wab_ref, wpg_ref, ps_ref, wpb_ref, wo_ref, g1_ref, b1_ref, wr_ref, br_ref,
                  h_ref, idx_ref, gate_ref, rank_ref, cnt_ref,
                  ext_ref, carry_ref, *, seq, tm):
    i = pl.program_id(0)
    tiles_per_seq = seq // tm
    jt = i % tiles_per_seq

    @pl.when(i == 0)
    def _():
        carry_ref[...] = jnp.zeros_like(carry_ref)

    ext_ref[0:POOL_HALO, :] = jnp.where(jt > 0, pp_ref[...], 0.0)
    ext_ref[POOL_HALO:POOL_HALO + tm, :] = pc_ref[...]
    ext_ref[POOL_HALO + tm:, :] = jnp.where(jt < tiles_per_seq - 1, pn_ref[...], 0.0)
    spos = jt * tm + lax.broadcasted_iota(jnp.int32, (tm, 1), 0)
    pooled_parts = []
    for g, w in enumerate(POOL_WINDOWS):
        cs = slice(g * POOL_GROUP_WIDTH, (g + 1) * POOL_GROUP_WIDTH)
        acc = None
        for off in range(-(w // 2), w // 2):
            t = ext_ref[POOL_HALO + off:POOL_HALO + off + tm, cs]
            acc = t if acc is None else acc + t
        hi = jnp.minimum(spos + (w // 2 - 1), seq - 1)
        lo = jnp.maximum(spos - w // 2, 0)
        cnt = (hi - lo + 1).astype(F32)
        y = acc / cnt - pc_ref[:, cs]
        yg = jnp.dot(y.astype(BF16), wpg_ref[g], preferred_element_type=F32)
        pooled_parts.append(yg * ps_ref[:, cs])
    pool_feat = jnp.concatenate(pooled_parts, axis=-1).astype(BF16)

    a_br = jnp.dot(a_ref[...], wab_ref[...], preferred_element_type=F32)
    p_br = jnp.dot(pool_feat, wpb_ref[...], preferred_element_type=F32)
    mixed = ga_ref[...] * a_br + gp_ref[...] * p_br
    o = jnp.dot(mixed.astype(BF16), wo_ref[...], preferred_element_type=F32)
    h = _layer_norm(DEEPNORM_ALPHA * x_ref[...] + o, g1_ref[...], b1_ref[...])
    h_ref[...] = h

    h_hi = h.astype(BF16)
    h_lo = (h - h_hi.astype(F32)).astype(BF16)
    w_hi = wr_ref[...].astype(BF16)
    w_lo = (wr_ref[...] - w_hi.astype(F32)).astype(BF16)
    nt = (((1,), (1,)), ((), ()))
    part = lax.dot_general(jnp.concatenate([w_hi, w_lo], axis=0), h_hi, nt,
                           preferred_element_type=F32)
    logits = (part[:N_EXPERTS] + part[N_EXPERTS:]
              + lax.dot_general(w_hi, h_lo, nt, preferred_element_type=F32)
              + br_ref[...])

    eidx = lax.broadcasted_iota(jnp.int32, (N_EXPERTS, tm), 0)
    work = logits
    sel_idx, sel_val, sel_hit = [], [], []
    chosen = jnp.zeros((N_EXPERTS, tm), F32)
    for _ in range(TOP_K):
        m = jnp.max(work, axis=0, keepdims=True)
        ix = jnp.min(jnp.where(work == m, eidx, N_EXPERTS), axis=0, keepdims=True)
        hit = eidx == ix
        sel_idx.append(ix)
        sel_val.append(m)
        sel_hit.append(hit)
        chosen = jnp.where(hit, 1.0, chosen)
        work = jnp.where(hit, -jnp.inf, work)
    ex = [jnp.exp(v - sel_val[0]) for v in sel_val]
    tot = ex[0] + ex[1] + ex[2] + ex[3]
    gate_ref[...] = jnp.concatenate([e / tot for e in ex], axis=0)
    idx_ref[...] = jnp.concatenate(sel_idx, axis=0)

    r_i = lax.broadcasted_iota(jnp.int32, (tm, tm), 0)
    c_i = lax.broadcasted_iota(jnp.int32, (tm, tm), 1)
    earlier = (r_i < c_i).astype(BF16)
    before = jnp.dot(chosen.astype(BF16), earlier, preferred_element_type=F32) + carry_ref[...]
    ranks = [jnp.sum(jnp.where(hit, before, 0.0), axis=0, keepdims=True) for hit in sel_hit]
    rank_ref[...] = jnp.concatenate(ranks, axis=0).astype(jnp.int32)
    carry_ref[...] = carry_ref[...] + jnp.sum(chosen, axis=1, keepdims=True)
    cnt_ref[...] = carry_ref[...].astype(jnp.int32)


def _merge(x2, attn, p_in, g_attn, g_pool, wab, wpg, ps, wpb, wo, g1, b1, wr, br, seq):
    n, d = x2.shape
    tm = TM_MERGE
    halo_blocks = tm // POOL_HALO
    row = lambda i: (i, 0)
    col = lambda i: (0, i)
    full2 = lambda i: (0, 0)

    def prev_halo(i):
        return (jnp.maximum(i * halo_blocks - 1, 0), 0)

    def next_halo(i):
        return (jnp.minimum((i + 1) * halo_blocks, n // POOL_HALO - 1), 0)

    return pl.pallas_call(
        functools.partial(_merge_kernel, seq=seq, tm=tm),
        grid=(n // tm,),
        in_specs=[pl.BlockSpec((tm, d), row),
                  pl.BlockSpec((tm, ATTN_WIDTH), row),
                  pl.BlockSpec((tm, POOL_WIDTH), row),
                  pl.BlockSpec((POOL_HALO, POOL_WIDTH), prev_halo),
                  pl.BlockSpec((POOL_HALO, POOL_WIDTH), next_halo),
                  pl.BlockSpec((tm, d), row),
                  pl.BlockSpec((tm, d), row),
                  pl.BlockSpec(wab.shape, full2),
                  pl.BlockSpec(wpg.shape, lambda i: (0, 0, 0)),
                  pl.BlockSpec(ps.shape, full2),
                  pl.BlockSpec(wpb.shape, full2),
                  pl.BlockSpec(wo.shape, full2),
                  pl.BlockSpec(g1.shape, full2),
                  pl.BlockSpec(b1.shape, full2),
                  pl.BlockSpec(wr.shape, full2),
                  pl.BlockSpec(br.shape, full2)],
        out_specs=[pl.BlockSpec((tm, d), row),
                   pl.BlockSpec((TOP_K, tm), col),
                   pl.BlockSpec((TOP_K, tm), col),
                   pl.BlockSpec((TOP_K, tm), col),
                   pl.BlockSpec((N_EXPERTS, 1), full2)],
        out_shape=[jax.ShapeDtypeStruct((n, d), F32),
                   jax.ShapeDtypeStruct((TOP_K, n), jnp.int32),
                   jax.ShapeDtypeStruct((TOP_K, n), F32),
                   jax.ShapeDtypeStruct((TOP_K, n), jnp.int32),
                   jax.ShapeDtypeStruct((N_EXPERTS, 1), jnp.int32)],
        scratch_shapes=[pltpu.VMEM((tm + 2 * POOL_HALO, POOL_WIDTH), F32),
                        pltpu.VMEM((N_EXPERTS, 1), F32)],
        compiler_params=pltpu.CompilerParams(
            dimension_semantics=("arbitrary",), vmem_limit_bytes=VMEM_LIMIT),
        name="merge",
    )(x2, attn, p_in, p_in, p_in, g_attn, g_pool, wab, wpg, ps, wpb, wo, g1, b1, wr, br)


def _dispatch_kernel(pos_ref, pend_ref, nu_ref, h_ref, xs_ref, zero_ref, sem, zsem, *, tm, bm, nblk):
    base = pl.program_id(0) * (tm * TOP_K)

    @pl.when(pl.program_id(0) == 0)
    def _():
        zero_ref[...] = jnp.zeros_like(zero_ref)

        def fill(start):
            return pltpu.make_async_copy(zero_ref, xs_ref.at[pl.ds(start, bm), :], zsem)

        def group_tail(e):
            prev_end = jnp.where(e > 0, pend_ref[jnp.maximum(e - 1, 0)], 0)
            return pend_ref[e] > prev_end, pl.multiple_of(pend_ref[e] - bm, bm)

        def per_expert(op):
            def body(e, c):
                nonempty, start = group_tail(e)

                @pl.when(nonempty)
                def _():
                    op(fill(start))
                return c
            lax.fori_loop(0, N_EXPERTS, body, 0)

        def per_tail_block(op):
            def body(b, c):
                op(fill(pl.multiple_of(b * bm, bm)))
                return c
            lax.fori_loop(nu_ref[0], nblk, body, 0)

        per_expert(lambda c: c.start())
        per_tail_block(lambda c: c.start())
        per_expert(lambda c: c.wait())
        per_tail_block(lambda c: c.wait())

    def row_copy(t, k):
        dst = pos_ref[base + t * TOP_K + k]
        return pltpu.make_async_copy(h_ref.at[pl.ds(t, 1), :], xs_ref.at[pl.ds(dst, 1), :], sem)

    def issue(tt, c):
        for u in range(ROW_UNROLL):
            for k in range(TOP_K):
                row_copy(tt * ROW_UNROLL + u, k).start(priority=k % 2)
        return c

    lax.fori_loop(0, tm // ROW_UNROLL, issue, 0)

    for _ in range(TOP_K):
        pltpu.make_async_copy(h_ref, xs_ref.at[pl.ds(0, tm), :], sem).wait()


def _dispatch(pos_flat, pend, n_used, h, n_rows):
    n, d = h.shape
    tm = TM_ROWS
    bm = BM_FFN
    return pl.pallas_call(
        functools.partial(_dispatch_kernel, tm=tm, bm=bm, nblk=n_rows // bm),
        grid_spec=pltpu.PrefetchScalarGridSpec(
            num_scalar_prefetch=3,
            grid=(n // tm,),
            in_specs=[pl.BlockSpec((tm, d), lambda i, p, e, u: (i, 0))],
            out_specs=pl.BlockSpec(memory_space=pl.ANY),
            scratch_shapes=[pltpu.VMEM((bm, d), F32),
                            pltpu.SemaphoreType.DMA,
                            pltpu.SemaphoreType.DMA]),
        out_shape=jax.ShapeDtypeStruct((n_rows, d), F32),
        compiler_params=pltpu.CompilerParams(
            dimension_semantics=("arbitrary",), vmem_limit_bytes=VMEM_LIMIT),
        name="dispatch",
    )(pos_flat, pend, n_used, h)


def _ffn_kernel(be_ref, nv_ref, nu_ref, x_ref, w1_ref, b1_ref, w2_ref, b2_ref, y_ref,
                w1b_ref, w2b_ref, *, bm, f):
    b = pl.program_id(0)
    used = b < nu_ref[0]
    new_expert = (b == 0) | (be_ref[b] != be_ref[jnp.maximum(b - 1, 0)])

    @pl.when(used & new_expert)
    def _():
        w1b_ref[...] = w1_ref[0].astype(BF16)
        w2b_ref[...] = w2_ref[0].astype(BF16)

    @pl.when(used)
    def _():
        rows = lax.broadcasted_iota(jnp.int32, (bm, 1), 0)
        x = jnp.where(rows < nv_ref[b], x_ref[...], 0.0).astype(BF16)
        h = jnp.dot(x, w1b_ref[...], preferred_element_type=F32) + b1_ref[0]
        gate = jnp.minimum(h[:, :f], SWIGLU_LIMIT)
        up = jnp.clip(h[:, f:], -SWIGLU_LIMIT, SWIGLU_LIMIT)
        act = (up + 1.0) * gate * _sigmoid(SWIGLU_ALPHA * gate)
        y_ref[...] = jnp.dot(act.astype(BF16), w2b_ref[...], preferred_element_type=F32) + b2_ref[0]

    @pl.when(b >= nu_ref[0])
    def _():
        y_ref[...] = jnp.zeros_like(y_ref)


def _ffn(blk_expert, blk_valid, n_used, xs, w1, b1, w2, b2):
    n_rows, d = xs.shape
    bm = BM_FFN
    f = w2.shape[1]

    def row_map(b, be, nv, nu):
        return (jnp.minimum(b, nu[0] - 1), 0)

    def exp_map(b, be, nv, nu):
        return (be[b], 0, 0)

    return pl.pallas_call(
        functools.partial(_ffn_kernel, bm=bm, f=f),
        grid_spec=pltpu.PrefetchScalarGridSpec(
            num_scalar_prefetch=3,
            grid=(n_rows // bm,),
            in_specs=[pl.BlockSpec((bm, d), row_map),
                      pl.BlockSpec((1, d, 2 * f), exp_map),
                      pl.BlockSpec((1, 1, 2 * f), exp_map),
                      pl.BlockSpec((1, f, d), exp_map),
                      pl.BlockSpec((1, 1, d), exp_map)],
            out_specs=pl.BlockSpec((bm, d), lambda b, be, nv, nu: (b, 0)),
            scratch_shapes=[pltpu.VMEM((d, 2 * f), BF16),
                            pltpu.VMEM((f, d), BF16)]),
        out_shape=jax.ShapeDtypeStruct((n_rows, d), F32),
        compiler_params=pltpu.CompilerParams(
            dimension_semantics=("arbitrary",), vmem_limit_bytes=VMEM_LIMIT),
        name="ffn",
    )(blk_expert, blk_valid, n_used, xs, w1, b1, w2, b2)


def _combine_kernel(pos_ref, h_ref, gate_ref, g2_ref, b2_ref, ys_ref, o_ref, buf_ref, sem, *, tm):
    base = pl.program_id(0) * (tm * TOP_K)

    def row_copy(t, k):
        src = pos_ref[base + t * TOP_K + k]
        return pltpu.make_async_copy(ys_ref.at[pl.ds(src, 1), :], buf_ref.at[k, pl.ds(t, 1), :], sem)

    def issue(tt, c):
        for u in range(ROW_UNROLL):
            for k in range(TOP_K):
                row_copy(tt * ROW_UNROLL + u, k).start(priority=k % 2)
        return c

    lax.fori_loop(0, tm // ROW_UNROLL, issue, 0)

    for k in range(TOP_K):
        pltpu.make_async_copy(ys_ref.at[pl.ds(0, tm), :], buf_ref.at[k], sem).wait()

    f = gate_ref[:, 0:1] * buf_ref[0]
    for k in range(1, TOP_K):
        f = f + gate_ref[:, k:k + 1] * buf_ref[k]
    o_ref[...] = _layer_norm(DEEPNORM_ALPHA * h_ref[...] + f, g2_ref[...], b2_ref[...])


def _combine(pos_flat, h, gates, g2, b2, ys):
    n, d = h.shape
    tm = TM_ROWS
    row = lambda i, p: (i, 0)
    full2 = lambda i, p: (0, 0)
    return pl.pallas_call(
        functools.partial(_combine_kernel, tm=tm),
        grid_spec=pltpu.PrefetchScalarGridSpec(
            num_scalar_prefetch=1,
            grid=(n // tm,),
            in_specs=[pl.BlockSpec((tm, d), row),
                      pl.BlockSpec((tm, TOP_K), row),
                      pl.BlockSpec(g2.shape, full2),
                      pl.BlockSpec(b2.shape, full2),
                      pl.BlockSpec(memory_space=pl.ANY)],
            out_specs=pl.BlockSpec((tm, d), row),
            scratch_shapes=[pltpu.VMEM((TOP_K, tm, d), F32),
                            pltpu.SemaphoreType.DMA]),
        out_shape=jax.ShapeDtypeStruct((n, d), F32),
        compiler_params=pltpu.CompilerParams(
            dimension_semantics=("arbitrary",), vmem_limit_bytes=VMEM_LIMIT),
        name="combine",
    )(pos_flat, h, gates, g2, b2, ys)


def _layer(h2, seq, w_in, sinks, wab, wpg, ps, wpb, wo, g1, b1, wr, br, w1, bb1, w2, bb2, g2, b2):
    n, d = h2.shape
    q, k, v, p_in, g_attn, g_pool = _inproj(h2, w_in.astype(BF16))
    attn = _attention(q, k, v, sinks, seq)
    h1, idx, gates, rank, counts = _merge(
        h2, attn, p_in, g_attn, g_pool,
        wab.astype(BF16), wpg.astype(BF16), ps.reshape(1, -1), wpb.astype(BF16), wo.astype(BF16),
        g1.reshape(1, -1), b1.reshape(1, -1), wr.T, br.reshape(-1, 1), seq)
    idx, gates, rank = idx.T, gates.T, rank.T

    bm = BM_FFN
    counts = counts.reshape(N_EXPERTS)
    padded = ((counts + bm - 1) // bm) * bm
    pend = jnp.cumsum(padded)
    pstart = pend - padded
    n_rows = n * TOP_K + N_EXPERTS * bm
    nblk = n_rows // bm
    blk_lo = jnp.arange(nblk, dtype=jnp.int32) * bm
    blk_expert = jnp.minimum(
        jnp.sum((blk_lo[:, None] >= pend[None, :]).astype(jnp.int32), axis=1), N_EXPERTS - 1)
    blk_valid = jnp.clip((pstart + counts)[blk_expert] - blk_lo, 0, bm).astype(jnp.int32)
    n_used = (pend[-1:] // bm).astype(jnp.int32)
    onehot = idx[:, :, None] == jnp.arange(N_EXPERTS, dtype=jnp.int32)[None, None, :]
    pos = rank + jnp.sum(jnp.where(onehot, pstart[None, None, :], 0), axis=-1)
    pos_flat = pos.reshape(n * TOP_K).astype(jnp.int32)

    xs = _dispatch(pos_flat, pend.astype(jnp.int32), n_used, h1, n_rows)
    ys = _ffn(blk_expert.astype(jnp.int32), blk_valid, n_used, xs,
              w1, bb1.reshape(N_EXPERTS, 1, -1), w2, bb2.reshape(N_EXPERTS, 1, -1))
    return _combine(pos_flat, h1, gates, g2.reshape(1, -1), b2.reshape(1, -1), ys)


def kernel(x, w_in, attn_sinks, w_attn_branch, w_pool_group, pool_scale, w_pool_branch, w_out,
           ln1_g, ln1_b, w_router, b_router, w_mlp1, b_mlp1, w_mlp2, b_mlp2, ln2_g, ln2_b):
    bsz, seq, d = x.shape
    h = x.reshape(bsz * seq, d)
    for l in range(w_in.shape[0]):
        h = _layer(h, seq, w_in[l], attn_sinks[l], w_attn_branch[l], w_pool_group[l], pool_scale[l],
                   w_pool_branch[l], w_out[l], ln1_g[l], ln1_b[l], w_router[l], b_router[l],
                   w_mlp1[l], b_mlp1[l], w_mlp2[l], b_mlp2[l], ln2_g[l], ln2_b[l])
    return h.reshape(bsz, seq, d)
```

```python
import functools
import math

import jax
import jax.numpy as jnp
import numpy as np
from jax import lax
from jax.experimental import pallas as pl
from jax.experimental.pallas import tpu as pltpu

F32 = jnp.float32
BF16 = jnp.bfloat16

N_Q_HEADS = 8
N_KV_HEADS = 2
HEAD_DIM = 64
GROUP = N_Q_HEADS // N_KV_HEADS
ATTN_WIDTH = N_Q_HEADS * HEAD_DIM
KV_WIDTH = N_KV_HEADS * HEAD_DIM
WINDOW = 128
BLOCK = 128
POOL_WINDOWS = (2, 4, 8, 16)
POOL_GROUP_WIDTH = 128
POOL_WIDTH = len(POOL_WINDOWS) * POOL_GROUP_WIDTH
POOL_HALO = 8
N_EXPERTS = 32
TOP_K = 4
SWIGLU_LIMIT = 7.0
SWIGLU_ALPHA = 1.702
LN_EPS = 1e-5
DEPTH = 1
DEEPNORM_ALPHA = (2.0 * DEPTH) ** 0.25
NEG_BIG = -1e30

TM_PROJ = 512
TM_MERGE = 512
SUBLANES = 8
LANES = 128
TM_ROWS = 512
ROW_UNROLL = 2
BM_FFN = 512
VMEM_LIMIT = 56 * 1024 * 1024


def _sigmoid(x):
    return 1.0 / (1.0 + jnp.exp(-x))


def _store_row_tiles(ref, value):
    rows = value.shape[0]
    for c in range(SUBLANES):
        ref[pl.ds(c, rows, stride=SUBLANES), :] = value[:, c * LANES:(c + 1) * LANES]


def _load_row_tiles(ref, rows):
    return jnp.concatenate(
        [ref[pl.ds(c, rows, stride=SUBLANES), :] for c in range(SUBLANES)], axis=-1)


def _layer_norm(x, g, b):
    mean = jnp.mean(x, axis=-1, keepdims=True)
    xc = x - mean
    var = jnp.mean(xc * xc, axis=-1, keepdims=True)
    return xc * lax.rsqrt(var + LN_EPS) * g + b


def _inproj_kernel(x_ref, w_ref, q_ref, k_ref, v_ref, p_ref, ga_ref, gp_ref, *, d_model):
    xb = x_ref[...].astype(BF16)
    o_k = ATTN_WIDTH
    o_v = o_k + KV_WIDTH
    o_p = o_v + KV_WIDTH
    o_g = o_p + POOL_WIDTH

    def proj(lo, hi):
        return jnp.dot(xb, w_ref[:, lo:hi], preferred_element_type=F32)

    q_ref[...] = (proj(0, o_k) * (1.0 / math.sqrt(HEAD_DIM))).astype(BF16)
    k_ref[...] = proj(o_k, o_v).astype(BF16)
    v_ref[...] = proj(o_v, o_p).astype(BF16)
    p_ref[...] = proj(o_p, o_g)
    ga_ref[...] = _sigmoid(proj(o_g, o_g + d_model))
    gp_ref[...] = _sigmoid(proj(o_g + d_model, o_g + 2 * d_model))


def _inproj(x2, w_in_b):
    n, d = x2.shape
    in_width = w_in_b.shape[1]
    tm = TM_PROJ
    row = lambda i: (i, 0)
    return pl.pallas_call(
        functools.partial(_inproj_kernel, d_model=d),
        grid=(n // tm,),
        in_specs=[pl.BlockSpec((tm, d), row),
                  pl.BlockSpec((d, in_width), lambda i: (0, 0))],
        out_specs=[pl.BlockSpec((tm, ATTN_WIDTH), row),
                   pl.BlockSpec((tm, KV_WIDTH), row),
                   pl.BlockSpec((tm, KV_WIDTH), row),
                   pl.BlockSpec((tm, POOL_WIDTH), row),
                   pl.BlockSpec((tm, d), row),
                   pl.BlockSpec((tm, d), row)],
        out_shape=[jax.ShapeDtypeStruct((n, ATTN_WIDTH), BF16),
                   jax.ShapeDtypeStruct((n, KV_WIDTH), BF16),
                   jax.ShapeDtypeStruct((n, KV_WIDTH), BF16),
                   jax.ShapeDtypeStruct((n, POOL_WIDTH), F32),
                   jax.ShapeDtypeStruct((n, d), F32),
                   jax.ShapeDtypeStruct((n, d), F32)],
        compiler_params=pltpu.CompilerParams(
            dimension_semantics=("arbitrary",), vmem_limit_bytes=VMEM_LIMIT),
        name="inproj",
    )(x2, w_in_b)


def _alibi_slope(h):
    return 2.0 ** (-8.0 * (h + 1) / N_Q_HEADS)


def _attn_kernel(sink_ref, q_ref, kp_ref, kc_ref, kn_ref, vp_ref, vc_ref, vn_ref, o_ref, *, nb):
    j = pl.program_id(0) % nb
    kk = jnp.concatenate([kp_ref[...], kc_ref[...], kn_ref[...]], axis=0)
    vv = jnp.concatenate([vp_ref[...], vc_ref[...], vn_ref[...]], axis=0)
    rows = GROUP * BLOCK
    ri = lax.broadcasted_iota(jnp.int32, (rows, 3 * BLOCK), 0)
    ki = lax.broadcasted_iota(jnp.int32, (rows, 3 * BLOCK), 1)
    arel = jnp.abs(ki - BLOCK - ri % BLOCK)
    valid = arel <= WINDOW
    valid = valid & ((ki >= BLOCK) | (j > 0)) & ((ki < 2 * BLOCK) | (j < nb - 1))
    arel_f = arel.astype(F32)
    grp = lax.broadcasted_iota(jnp.int32, (rows, 1), 0) // BLOCK

    def per_group_column(values):
        col = jnp.full((rows, 1), values[GROUP - 1], F32)
        for g in range(GROUP - 2, -1, -1):
            col = jnp.where(grp == g, values[g], col)
        return col

    for kvh in range(N_KV_HEADS):
        heads = [kvh * GROUP + g for g in range(GROUP)]
        slope = per_group_column([_alibi_slope(h) for h in heads])
        sink = per_group_column([sink_ref[h] for h in heads])
        qs = jnp.concatenate([q_ref[:, h * HEAD_DIM:(h + 1) * HEAD_DIM] for h in heads], axis=0)
        kh = kk[:, kvh * HEAD_DIM:(kvh + 1) * HEAD_DIM]
        vh = vv[:, kvh * HEAD_DIM:(kvh + 1) * HEAD_DIM]
        s = lax.dot_general(qs, kh, (((1,), (1,)), ((), ())), preferred_element_type=F32)
        s = jnp.where(valid, s - slope * arel_f, NEG_BIG)
        m = jnp.maximum(jnp.max(s, axis=-1, keepdims=True), sink)
        p = jnp.exp(s - m)
        denom = jnp.sum(p, axis=-1, keepdims=True) + jnp.exp(sink - m)
        o = (jnp.dot(p.astype(BF16), vh, preferred_element_type=F32) / denom).astype(BF16)
        for g, h in enumerate(heads):
            o_ref[:, h * HEAD_DIM:(h + 1) * HEAD_DIM] = o[g * BLOCK:(g + 1) * BLOCK]


def _attention(q, k, v, sinks, seq):
    n = q.shape[0]
    nb = seq // BLOCK
    nblk = n // BLOCK

    def prev_map(i, s):
        return (jnp.where(i % nb == 0, i, i - 1), 0)

    def next_map(i, s):
        return (jnp.where(i % nb == nb - 1, i, i + 1), 0)

    cur_map = lambda i, s: (i, 0)
    kv_spec = lambda m: pl.BlockSpec((BLOCK, KV_WIDTH), m)
    return pl.pallas_call(
        functools.partial(_attn_kernel, nb=nb),
        grid_spec=pltpu.PrefetchScalarGridSpec(
            num_scalar_prefetch=1,
            grid=(nblk,),
            in_specs=[pl.BlockSpec((BLOCK, ATTN_WIDTH), cur_map),
                      kv_spec(prev_map), kv_spec(cur_map), kv_spec(next_map),
                      kv_spec(prev_map), kv_spec(cur_map), kv_spec(next_map)],
            out_specs=pl.BlockSpec((BLOCK, ATTN_WIDTH), cur_map)),
        out_shape=jax.ShapeDtypeStruct((n, ATTN_WIDTH), BF16),
        compiler_params=pltpu.CompilerParams(
            dimension_semantics=("arbitrary",), vmem_limit_bytes=VMEM_LIMIT),
        name="attn",
    )(sinks, q, k, k, k, v, v, v)


def _merge_kernel(x_ref, a_ref, pc_ref, pp_ref, pn_ref, ga_ref, gp_ref,
                  wab_ref, wpg_ref, ps_ref, wpb_ref, wo_ref, g1_ref, b1_ref, wr_ref, br_ref,
                  h_ref, idx_ref, gate_ref, rank_ref, cnt_ref,
                  ext_ref, carry_ref, earlier_ref, *, seq, tm):
    i = pl.program_id(0)
    tiles_per_seq = seq // tm
    jt = i % tiles_per_seq

    @pl.when(i == 0)
    def _():
        carry_ref[...] = jnp.zeros_like(carry_ref)
        r_i = lax.broadcasted_iota(jnp.int32, (tm, tm), 0)
        c_i = lax.broadcasted_iota(jnp.int32, (tm, tm), 1)
        earlier_ref[...] = (r_i < c_i).astype(BF16)

    ext_ref[0:POOL_HALO, :] = jnp.where(jt > 0, pp_ref[...], 0.0)
    ext_ref[POOL_HALO:POOL_HALO + tm, :] = pc_ref[...]
    ext_ref[POOL_HALO + tm:, :] = jnp.where(jt < tiles_per_seq - 1, pn_ref[...], 0.0)
    spos = jt * tm + lax.broadcasted_iota(jnp.int32, (tm, 1), 0)
    pooled_parts = []
    for g, w in enumerate(POOL_WINDOWS):
        cs = slice(g * POOL_GROUP_WIDTH, (g + 1) * POOL_GROUP_WIDTH)
        acc = None
        for off in range(-(w // 2), w // 2):
            t = ext_ref[POOL_HALO + off:POOL_HALO + off + tm, cs]
            acc = t if acc is None else acc + t
        hi = jnp.minimum(spos + (w // 2 - 1), seq - 1)
        lo = jnp.maximum(spos - w // 2, 0)
        cnt = (hi - lo + 1).astype(F32)
        y = acc / cnt - pc_ref[:, cs]
        yg = jnp.dot(y.astype(BF16), wpg_ref[g], preferred_element_type=F32)
        pooled_parts.append(yg * ps_ref[:, cs])
    pool_feat = jnp.concatenate(pooled_parts, axis=-1).astype(BF16)

    a_br = jnp.dot(a_ref[...], wab_ref[...], preferred_element_type=F32)
    p_br = jnp.dot(pool_feat, wpb_ref[...], preferred_element_type=F32)
    mixed = ga_ref[...] * a_br + gp_ref[...] * p_br
    o = jnp.dot(mixed.astype(BF16), wo_ref[...], preferred_element_type=F32)
    h = _layer_norm(DEEPNORM_ALPHA * x_ref[...] + o, g1_ref[...], b1_ref[...])
    _store_row_tiles(h_ref, h)

    h_hi = h.astype(BF16)
    h_lo = (h - h_hi.astype(F32)).astype(BF16)
    w_hi = wr_ref[...].astype(BF16)
    w_lo = (wr_ref[...] - w_hi.astype(F32)).astype(BF16)
    nt = (((1,), (1,)), ((), ()))
    part = lax.dot_general(jnp.concatenate([w_hi, w_lo], axis=0), h_hi, nt,
                           preferred_element_type=F32)
    logits = (part[:N_EXPERTS] + part[N_EXPERTS:]
              + lax.dot_general(w_hi, h_lo, nt, preferred_element_type=F32)
              + br_ref[...])

    eidx = lax.broadcasted_iota(jnp.int32, (N_EXPERTS, tm), 0)
    work = logits
    sel_idx, sel_val, sel_hit = [], [], []
    chosen = jnp.zeros((N_EXPERTS, tm), F32)
    for _ in range(TOP_K):
        m = jnp.max(work, axis=0, keepdims=True)
        ix = jnp.min(jnp.where(work == m, eidx, N_EXPERTS), axis=0, keepdims=True)
        hit = eidx == ix
        sel_idx.append(ix)
        sel_val.append(m)
        sel_hit.append(hit)
        chosen = jnp.where(hit, 1.0, chosen)
        work = jnp.where(hit, -jnp.inf, work)
    ex = [jnp.exp(v - sel_val[0]) for v in sel_val]
    tot = ex[0] + ex[1] + ex[2] + ex[3]
    gate_ref[...] = jnp.concatenate([e / tot for e in ex], axis=0)
    idx_ref[...] = jnp.concatenate(sel_idx, axis=0)

    before = jnp.dot(chosen.astype(BF16), earlier_ref[...], preferred_element_type=F32) + carry_ref[...]
    ranks = [jnp.sum(jnp.where(hit, before, 0.0), axis=0, keepdims=True) for hit in sel_hit]
    rank_ref[...] = jnp.concatenate(ranks, axis=0).astype(jnp.int32)
    carry_ref[...] = carry_ref[...] + jnp.sum(chosen, axis=1, keepdims=True)
    cnt_ref[...] = carry_ref[...].astype(jnp.int32)


def _merge(x2, attn, p_in, g_attn, g_pool, wab, wpg, ps, wpb, wo, g1, b1, wr, br, seq):
    n, d = x2.shape
    tm = TM_MERGE
    halo_blocks = tm // POOL_HALO
    row = lambda i: (i, 0)
    col = lambda i: (0, i)
    full2 = lambda i: (0, 0)

    def prev_halo(i):
        return (jnp.maximum(i * halo_blocks - 1, 0), 0)

    def next_halo(i):
        return (jnp.minimum((i + 1) * halo_blocks, n // POOL_HALO - 1), 0)

    return pl.pallas_call(
        functools.partial(_merge_kernel, seq=seq, tm=tm),
        grid=(n // tm,),
        in_specs=[pl.BlockSpec((tm, d), row),
                  pl.BlockSpec((tm, ATTN_WIDTH), row),
                  pl.BlockSpec((tm, POOL_WIDTH), row),
                  pl.BlockSpec((POOL_HALO, POOL_WIDTH), prev_halo),
                  pl.BlockSpec((POOL_HALO, POOL_WIDTH), next_halo),
                  pl.BlockSpec((tm, d), row),
                  pl.BlockSpec((tm, d), row),
                  pl.BlockSpec(wab.shape, full2),
                  pl.BlockSpec(wpg.shape, lambda i: (0, 0, 0)),
                  pl.BlockSpec(ps.shape, full2),
                  pl.BlockSpec(wpb.shape, full2),
                  pl.BlockSpec(wo.shape, full2),
                  pl.BlockSpec(g1.shape, full2),
                  pl.BlockSpec(b1.shape, full2),
                  pl.BlockSpec(wr.shape, full2),
                  pl.BlockSpec(br.shape, full2)],
        out_specs=[pl.BlockSpec((tm * SUBLANES, LANES), row),
                   pl.BlockSpec((TOP_K, tm), col),
                   pl.BlockSpec((TOP_K, tm), col),
                   pl.BlockSpec((TOP_K, tm), col),
                   pl.BlockSpec((N_EXPERTS, 1), full2)],
        out_shape=[jax.ShapeDtypeStruct((n * SUBLANES, LANES), F32),
                   jax.ShapeDtypeStruct((TOP_K, n), jnp.int32),
                   jax.ShapeDtypeStruct((TOP_K, n), F32),
                   jax.ShapeDtypeStruct((TOP_K, n), jnp.int32),
                   jax.ShapeDtypeStruct((N_EXPERTS, 1), jnp.int32)],
        scratch_shapes=[pltpu.VMEM((tm + 2 * POOL_HALO, POOL_WIDTH), F32),
                        pltpu.VMEM((N_EXPERTS, 1), F32),
                        pltpu.VMEM((tm, tm), BF16)],
        compiler_params=pltpu.CompilerParams(
            dimension_semantics=("arbitrary",), vmem_limit_bytes=VMEM_LIMIT),
        name="merge",
    )(x2, attn, p_in, p_in, p_in, g_attn, g_pool, wab, wpg, ps, wpb, wo, g1, b1, wr, br)


def _dispatch_kernel(pos_ref, pend_ref, nu_ref, h_ref, xs_ref, zero_ref, sem, zsem, *, tm, bm, nblk):
    base = pl.program_id(0) * (tm * TOP_K)

    @pl.when(pl.program_id(0) == 0)
    def _():
        zero_ref[...] = jnp.zeros_like(zero_ref)

        def fill(start):
            return pltpu.make_async_copy(zero_ref, xs_ref.at[pl.ds(start, bm)], zsem)

        def group_tail(e):
            prev_end = jnp.where(e > 0, pend_ref[jnp.maximum(e - 1, 0)], 0)
            return pend_ref[e] > prev_end, pl.multiple_of(pend_ref[e] - bm, bm)

        def per_expert(op):
            def body(e, c):
                nonempty, start = group_tail(e)

                @pl.when(nonempty)
                def _():
                    op(fill(start))
                return c
            lax.fori_loop(0, N_EXPERTS, body, 0)

        def per_tail_block(op):
            def body(b, c):
                op(fill(pl.multiple_of(b * bm, bm)))
                return c
            lax.fori_loop(nu_ref[0], nblk, body, 0)

        per_expert(lambda c: c.start())
        per_tail_block(lambda c: c.start())
        per_expert(lambda c: c.wait())
        per_tail_block(lambda c: c.wait())

    def row_copy(t, k):
        dst = pos_ref[base + t * TOP_K + k]
        return pltpu.make_async_copy(h_ref.at[t], xs_ref.at[dst], sem)

    def issue(tt, c):
        for u in range(ROW_UNROLL):
            for k in range(TOP_K):
                row_copy(tt * ROW_UNROLL + u, k).start(priority=k % 2)
        return c

    lax.fori_loop(0, tm // ROW_UNROLL, issue, 0)

    for _ in range(TOP_K):
        pltpu.make_async_copy(h_ref, xs_ref.at[pl.ds(0, tm)], sem).wait()


def _dispatch(pos_flat, pend, n_used, h_tiles, n_rows):
    n = h_tiles.shape[0]
    tm = TM_ROWS
    bm = BM_FFN
    return pl.pallas_call(
        functools.partial(_dispatch_kernel, tm=tm, bm=bm, nblk=n_rows // bm),
        grid_spec=pltpu.PrefetchScalarGridSpec(
            num_scalar_prefetch=3,
            grid=(n // tm,),
            in_specs=[pl.BlockSpec((tm, SUBLANES, LANES), lambda i, p, e, u: (i, 0, 0))],
            out_specs=pl.BlockSpec(memory_space=pl.ANY),
            scratch_shapes=[pltpu.VMEM((bm, SUBLANES, LANES), F32),
                            pltpu.SemaphoreType.DMA,
                            pltpu.SemaphoreType.DMA]),
        out_shape=jax.ShapeDtypeStruct((n_rows, SUBLANES, LANES), F32),
        compiler_params=pltpu.CompilerParams(
            dimension_semantics=("arbitrary",), vmem_limit_bytes=VMEM_LIMIT),
        name="dispatch",
    )(pos_flat, pend, n_used, h_tiles)


def _ffn_kernel(be_ref, nv_ref, first_ref, slot_ref, nxt_ref, nu_ref,
                x_ref, w1_hbm, b1_ref, w2_hbm, b2_ref, y_ref,
                w1f_ref, w2f_ref, w1b_ref, w2b_ref, wsem, *, bm, f):
    b = pl.program_id(0)
    used = b < nu_ref[0]

    def fetch(e, s):
        return (pltpu.make_async_copy(w1_hbm.at[e], w1f_ref.at[s], wsem.at[0, s]),
                pltpu.make_async_copy(w2_hbm.at[e], w2f_ref.at[s], wsem.at[1, s]))

    @pl.when(b == 0)
    def _():
        for c in fetch(be_ref[0], 0):
            c.start()

    @pl.when(used & (first_ref[b] == 1))
    def _():
        s = slot_ref[b]
        for c in fetch(be_ref[b], s):
            c.wait()

        @pl.when(nxt_ref[b] >= 0)
        def _():
            for c in fetch(nxt_ref[b], 1 - s):
                c.start()

        w1b_ref[...] = w1f_ref[s].astype(BF16)
        w2b_ref[...] = w2f_ref[s].astype(BF16)

    @pl.when(used)
    def _():
        rows = lax.broadcasted_iota(jnp.int32, (bm, 1), 0)
        x = jnp.where(rows < nv_ref[b], _load_row_tiles(x_ref, bm), 0.0).astype(BF16)
        h = jnp.dot(x, w1b_ref[...], preferred_element_type=F32) + b1_ref[0]
        gate = jnp.minimum(h[:, :f], SWIGLU_LIMIT)
        up = jnp.clip(h[:, f:], -SWIGLU_LIMIT, SWIGLU_LIMIT)
        act = (up + 1.0) * gate * _sigmoid(SWIGLU_ALPHA * gate)
        y = jnp.dot(act.astype(BF16), w2b_ref[...], preferred_element_type=F32) + b2_ref[0]
        _store_row_tiles(y_ref, y)

    @pl.when(b >= nu_ref[0])
    def _():
        y_ref[...] = jnp.zeros_like(y_ref)


def _ffn(blk_expert, blk_valid, blk_first, blk_slot, blk_next, n_used, xs_tiles, w1, b1, w2, b2):
    n_rows = xs_tiles.shape[0] // SUBLANES
    bm = BM_FFN
    d, f = w1.shape[1], w2.shape[1]

    def row_map(b, be, nv, fi, sl, nx, nu):
        return (jnp.minimum(b, nu[0] - 1), 0)

    def exp_map(b, be, nv, fi, sl, nx, nu):
        return (be[b], 0, 0)

    return pl.pallas_call(
        functools.partial(_ffn_kernel, bm=bm, f=f),
        grid_spec=pltpu.PrefetchScalarGridSpec(
            num_scalar_prefetch=6,
            grid=(n_rows // bm,),
            in_specs=[pl.BlockSpec((bm * SUBLANES, LANES), row_map),
                      pl.BlockSpec(memory_space=pl.ANY),
                      pl.BlockSpec((1, 1, 2 * f), exp_map),
                      pl.BlockSpec(memory_space=pl.ANY),
                      pl.BlockSpec((1, 1, d), exp_map)],
            out_specs=pl.BlockSpec((bm * SUBLANES, LANES), lambda b, *_: (b, 0)),
            scratch_shapes=[pltpu.VMEM((2, d, 2 * f), F32),
                            pltpu.VMEM((2, f, d), F32),
                            pltpu.VMEM((d, 2 * f), BF16),
                            pltpu.VMEM((f, d), BF16),
                            pltpu.SemaphoreType.DMA((2, 2))]),
        out_shape=jax.ShapeDtypeStruct((n_rows * SUBLANES, LANES), F32),
        compiler_params=pltpu.CompilerParams(
            dimension_semantics=("arbitrary",), vmem_limit_bytes=VMEM_LIMIT),
        name="ffn",
    )(blk_expert, blk_valid, blk_first, blk_slot, blk_next, n_used, xs_tiles, w1, b1, w2, b2)


def _combine_kernel(pos_ref, h_ref, gate_ref, g2_ref, b2_ref, ys_ref, o_ref, buf_ref, sem, *, tm, nsteps):
    i = pl.program_id(0)
    slot = i % 2

    def request_tile(step, s):
        base = step * (tm * TOP_K)

        def row_copy(t, k):
            src = pos_ref[base + t * TOP_K + k]
            dst = buf_ref.at[s, k, pl.ds(pl.multiple_of(t * SUBLANES, SUBLANES), SUBLANES), :]
            return pltpu.make_async_copy(ys_ref.at[src], dst, sem.at[s])

        def issue(tt, c):
            for u in range(ROW_UNROLL):
                for k in range(TOP_K):
                    row_copy(tt * ROW_UNROLL + u, k).start(priority=k % 2)
            return c

        lax.fori_loop(0, tm // ROW_UNROLL, issue, 0)

    @pl.when(i == 0)
    def _():
        request_tile(0, 0)

    for k in range(TOP_K):
        pltpu.make_async_copy(h_ref, buf_ref.at[slot, k], sem.at[slot]).wait()

    @pl.when(i + 1 < nsteps)
    def _():
        request_tile(i + 1, 1 - slot)

    f = gate_ref[:, 0:1] * _load_row_tiles(buf_ref.at[slot, 0], tm)
    for k in range(1, TOP_K):
        f = f + gate_ref[:, k:k + 1] * _load_row_tiles(buf_ref.at[slot, k], tm)
    h = _load_row_tiles(h_ref, tm)
    o_ref[...] = _layer_norm(DEEPNORM_ALPHA * h + f, g2_ref[...], b2_ref[...])


def _combine(pos_flat, h_tiles, gates, g2, b2, ys_tiles):
    n = h_tiles.shape[0] // SUBLANES
    d = SUBLANES * LANES
    tm = TM_ROWS
    row = lambda i, p: (i, 0)
    full2 = lambda i, p: (0, 0)
    return pl.pallas_call(
        functools.partial(_combine_kernel, tm=tm, nsteps=n // tm),
        grid_spec=pltpu.PrefetchScalarGridSpec(
            num_scalar_prefetch=1,
            grid=(n // tm,),
            in_specs=[pl.BlockSpec((tm * SUBLANES, LANES), row),
                      pl.BlockSpec((tm, TOP_K), row),
                      pl.BlockSpec(g2.shape, full2),
                      pl.BlockSpec(b2.shape, full2),
                      pl.BlockSpec(memory_space=pl.ANY)],
            out_specs=pl.BlockSpec((tm, d), row),
            scratch_shapes=[pltpu.VMEM((2, TOP_K, tm * SUBLANES, LANES), F32),
                            pltpu.SemaphoreType.DMA((2,))]),
        out_shape=jax.ShapeDtypeStruct((n, d), F32),
        compiler_params=pltpu.CompilerParams(
            dimension_semantics=("arbitrary",), vmem_limit_bytes=VMEM_LIMIT),
        name="combine",
    )(pos_flat, h_tiles, gates, g2, b2, ys_tiles)


def _layer(h2, seq, w_in, sinks, wab, wpg, ps, wpb, wo, g1, b1, wr, br, w1, bb1, w2, bb2, g2, b2):
    n, d = h2.shape
    q, k, v, p_in, g_attn, g_pool = _inproj(h2, w_in.astype(BF16))
    attn = _attention(q, k, v, sinks, seq)
    h1, idx, gates, rank, counts = _merge(
        h2, attn, p_in, g_attn, g_pool,
        wab.astype(BF16), wpg.astype(BF16), ps.reshape(1, -1), wpb.astype(BF16), wo.astype(BF16),
        g1.reshape(1, -1), b1.reshape(1, -1), wr.T, br.reshape(-1, 1), seq)
    idx, gates, rank = idx.T, gates.T, rank.T

    bm = BM_FFN
    counts = counts.reshape(N_EXPERTS)
    padded = ((counts + bm - 1) // bm) * bm
    pend = jnp.cumsum(padded)
    pstart = pend - padded
    n_rows = n * TOP_K + N_EXPERTS * bm
    nblk = n_rows // bm
    blk_lo = jnp.arange(nblk, dtype=jnp.int32) * bm
    blk_expert = jnp.minimum(
        jnp.sum((blk_lo[:, None] >= pend[None, :]).astype(jnp.int32), axis=1), N_EXPERTS - 1)
    blk_valid = jnp.clip((pstart + counts)[blk_expert] - blk_lo, 0, bm).astype(jnp.int32)
    n_used = (pend[-1:] // bm).astype(jnp.int32)
    experts = jnp.arange(N_EXPERTS, dtype=jnp.int32)
    nonempty = counts > 0
    group_slot = (jnp.cumsum(nonempty.astype(jnp.int32)) - 1) % 2
    later = jnp.where(nonempty[None, :] & (experts[None, :] > experts[:, None]), experts[None, :], N_EXPERTS)
    next_group = jnp.min(later, axis=1)
    next_group = jnp.where(next_group == N_EXPERTS, -1, next_group)
    blk_first = (blk_lo == pstart[blk_expert]).astype(jnp.int32)
    blk_slot = group_slot[blk_expert].astype(jnp.int32)
    blk_next = next_group[blk_expert].astype(jnp.int32)

    onehot = idx[:, :, None] == experts[None, None, :]
    pos = rank + jnp.sum(jnp.where(onehot, pstart[None, None, :], 0), axis=-1)
    pos_flat = pos.reshape(n * TOP_K).astype(jnp.int32)

    xs = _dispatch(pos_flat, pend.astype(jnp.int32), n_used,
                   h1.reshape(n, SUBLANES, LANES), n_rows)
    ys = _ffn(blk_expert.astype(jnp.int32), blk_valid, blk_first, blk_slot, blk_next, n_used,
              xs.reshape(n_rows * SUBLANES, LANES),
              w1, bb1.reshape(N_EXPERTS, 1, -1), w2, bb2.reshape(N_EXPERTS, 1, -1))
    return _combine(pos_flat, h1, gates, g2.reshape(1, -1), b2.reshape(1, -1),
                    ys.reshape(n_rows, SUBLANES, LANES))


def kernel(x, w_in, attn_sinks, w_attn_branch, w_pool_group, pool_scale, w_pool_branch, w_out,
           ln1_g, ln1_b, w_router, b_router, w_mlp1, b_mlp1, w_mlp2, b_mlp2, ln2_g, ln2_b):
    bsz, seq, d = x.shape
    assert d == SUBLANES * LANES, "row-tile layout needs one (8, 128) f32 tile per token row"
    h = x.reshape(bsz * seq, d)
    for l in range(w_in.shape[0]):
        h = _layer(h, seq, w_in[l], attn_sinks[l], w_attn_branch[l], w_pool_group[l], pool_scale[l],
                   w_pool_branch[l], w_out[l], ln1_g[l], ln1_b[l], w_router[l], b_router[l],
                   w_mlp1[l], b_mlp1[l], w_mlp2[l], b_mlp2[l], ln2_g[l], ln2_b[l])
    return h.reshape(bsz, seq, d)
```

```python
import functools
import math

import jax
import jax.numpy as jnp
import numpy as np
from jax import lax
from jax.experimental import pallas as pl
from jax.experimental.pallas import tpu as pltpu

F32 = jnp.float32
BF16 = jnp.bfloat16

N_Q_HEADS = 8
N_KV_HEADS = 2
HEAD_DIM = 64
GROUP = N_Q_HEADS // N_KV_HEADS
ATTN_WIDTH = N_Q_HEADS * HEAD_DIM
KV_WIDTH = N_KV_HEADS * HEAD_DIM
WINDOW = 128
BLOCK = 128
POOL_WINDOWS = (2, 4, 8, 16)
POOL_GROUP_WIDTH = 128
POOL_WIDTH = len(POOL_WINDOWS) * POOL_GROUP_WIDTH
POOL_HALO = 8
N_EXPERTS = 32
TOP_K = 4
SWIGLU_LIMIT = 7.0
SWIGLU_ALPHA = 1.702
LN_EPS = 1e-5
DEPTH = 1
DEEPNORM_ALPHA = (2.0 * DEPTH) ** 0.25
NEG_BIG = -1e30

TM_PROJ = 512
TM_MERGE = 512
TC_MERGE = 512
SUBLANES = 8
LANES = 128
TM_ROWS = 512
ROW_UNROLL = 2
BM_FFN = 512
VMEM_LIMIT = 56 * 1024 * 1024


def _sigmoid(x):
    return 1.0 / (1.0 + jnp.exp(-x))


def _store_row_tiles(ref, value):
    rows = value.shape[0]
    for c in range(SUBLANES):
        ref[pl.ds(c, rows, stride=SUBLANES), :] = value[:, c * LANES:(c + 1) * LANES]


def _load_row_tiles(ref, rows):
    return jnp.concatenate(
        [ref[pl.ds(c, rows, stride=SUBLANES), :] for c in range(SUBLANES)], axis=-1)


def _layer_norm(x, g, b):
    mean = jnp.mean(x, axis=-1, keepdims=True)
    xc = x - mean
    var = jnp.mean(xc * xc, axis=-1, keepdims=True)
    return xc * lax.rsqrt(var + LN_EPS) * g + b


def _inproj_kernel(x_ref, w_ref, q_ref, k_ref, v_ref, p_ref, ga_ref, gp_ref, *, d_model):
    xb = x_ref[...].astype(BF16)
    o_k = ATTN_WIDTH
    o_v = o_k + KV_WIDTH
    o_p = o_v + KV_WIDTH
    o_g = o_p + POOL_WIDTH

    def proj(lo, hi):
        return jnp.dot(xb, w_ref[:, lo:hi], preferred_element_type=F32)

    q_ref[...] = (proj(0, o_k) * (1.0 / math.sqrt(HEAD_DIM))).astype(BF16)
    k_ref[...] = proj(o_k, o_v).astype(BF16)
    v_ref[...] = proj(o_v, o_p).astype(BF16)
    p_ref[...] = proj(o_p, o_g)
    ga_ref[...] = _sigmoid(proj(o_g, o_g + d_model))
    gp_ref[...] = _sigmoid(proj(o_g + d_model, o_g + 2 * d_model))


def _inproj(x2, w_in_b):
    n, d = x2.shape
    in_width = w_in_b.shape[1]
    tm = TM_PROJ
    row = lambda i: (i, 0)
    return pl.pallas_call(
        functools.partial(_inproj_kernel, d_model=d),
        grid=(n // tm,),
        in_specs=[pl.BlockSpec((tm, d), row),
                  pl.BlockSpec((d, in_width), lambda i: (0, 0))],
        out_specs=[pl.BlockSpec((tm, ATTN_WIDTH), row),
                   pl.BlockSpec((tm, KV_WIDTH), row),
                   pl.BlockSpec((tm, KV_WIDTH), row),
                   pl.BlockSpec((tm, POOL_WIDTH), row),
                   pl.BlockSpec((tm, d), row),
                   pl.BlockSpec((tm, d), row)],
        out_shape=[jax.ShapeDtypeStruct((n, ATTN_WIDTH), BF16),
                   jax.ShapeDtypeStruct((n, KV_WIDTH), BF16),
                   jax.ShapeDtypeStruct((n, KV_WIDTH), BF16),
                   jax.ShapeDtypeStruct((n, POOL_WIDTH), F32),
                   jax.ShapeDtypeStruct((n, d), F32),
                   jax.ShapeDtypeStruct((n, d), F32)],
        compiler_params=pltpu.CompilerParams(
            dimension_semantics=("arbitrary",), vmem_limit_bytes=VMEM_LIMIT),
        name="inproj",
    )(x2, w_in_b)


def _alibi_slope(h):
    return 2.0 ** (-8.0 * (h + 1) / N_Q_HEADS)


def _attn_kernel(sink_ref, q_ref, kp_ref, kc_ref, kn_ref, vp_ref, vc_ref, vn_ref, o_ref,
                 bias_ref, *, nb):
    j = pl.program_id(0) % nb
    cols = GROUP * BLOCK
    keys = 3 * BLOCK

    @pl.when(pl.program_id(0) == 0)
    def _():
        ki = lax.broadcasted_iota(jnp.int32, (keys, cols), 0)
        ci = lax.broadcasted_iota(jnp.int32, (keys, cols), 1)
        arel = jnp.abs(ki - BLOCK - ci % BLOCK)
        arel_f = arel.astype(F32)
        grp = ci // BLOCK
        inside = [(arel <= WINDOW) & (ki >= BLOCK), arel <= WINDOW, (arel <= WINDOW) & (ki < 2 * BLOCK)]
        for kvh in range(N_KV_HEADS):
            slope = jnp.full((keys, cols), _alibi_slope(kvh * GROUP + GROUP - 1), F32)
            for g in range(GROUP - 2, -1, -1):
                slope = jnp.where(grp == g, _alibi_slope(kvh * GROUP + g), slope)
            for edge in range(3):
                bias_ref[edge, kvh] = jnp.where(inside[edge], -(slope * arel_f), NEG_BIG)

    edge = jnp.where(j == 0, 0, jnp.where(j == nb - 1, 2, 1))
    kk = jnp.concatenate([kp_ref[...], kc_ref[...], kn_ref[...]], axis=0)
    vv = jnp.concatenate([vp_ref[...], vc_ref[...], vn_ref[...]], axis=0)
    grp_row = lax.broadcasted_iota(jnp.int32, (1, cols), 1) // BLOCK

    for kvh in range(N_KV_HEADS):
        heads = [kvh * GROUP + g for g in range(GROUP)]
        sink = jnp.full((1, cols), sink_ref[heads[GROUP - 1]], F32)
        for g in range(GROUP - 2, -1, -1):
            sink = jnp.where(grp_row == g, sink_ref[heads[g]], sink)
        qs = jnp.concatenate([q_ref[:, h * HEAD_DIM:(h + 1) * HEAD_DIM] for h in heads], axis=0)
        kh = kk[:, kvh * HEAD_DIM:(kvh + 1) * HEAD_DIM]
        vh = vv[:, kvh * HEAD_DIM:(kvh + 1) * HEAD_DIM]
        s = lax.dot_general(kh, qs, (((1,), (1,)), ((), ())), preferred_element_type=F32)
        s = s + bias_ref[edge, kvh]
        m = jnp.maximum(jnp.max(s, axis=0, keepdims=True), sink)
        p = jnp.exp(s - m)
        denom = jnp.sum(p, axis=0, keepdims=True) + jnp.exp(sink - m)
        ot = lax.dot_general(vh, p.astype(BF16), (((0,), (0,)), ((), ())),
                             preferred_element_type=F32) / denom
        o = ot.T.astype(BF16)
        for g, h in enumerate(heads):
            o_ref[:, h * HEAD_DIM:(h + 1) * HEAD_DIM] = o[g * BLOCK:(g + 1) * BLOCK]


def _attention(q, k, v, sinks, seq):
    n = q.shape[0]
    nb = seq // BLOCK
    nblk = n // BLOCK

    def prev_map(i, s):
        return (jnp.where(i % nb == 0, i, i - 1), 0)

    def next_map(i, s):
        return (jnp.where(i % nb == nb - 1, i, i + 1), 0)

    cur_map = lambda i, s: (i, 0)
    kv_spec = lambda m: pl.BlockSpec((BLOCK, KV_WIDTH), m)
    return pl.pallas_call(
        functools.partial(_attn_kernel, nb=nb),
        grid_spec=pltpu.PrefetchScalarGridSpec(
            num_scalar_prefetch=1,
            grid=(nblk,),
            in_specs=[pl.BlockSpec((BLOCK, ATTN_WIDTH), cur_map),
                      kv_spec(prev_map), kv_spec(cur_map), kv_spec(next_map),
                      kv_spec(prev_map), kv_spec(cur_map), kv_spec(next_map)],
            out_specs=pl.BlockSpec((BLOCK, ATTN_WIDTH), cur_map),
            scratch_shapes=[pltpu.VMEM((3, N_KV_HEADS, 3 * BLOCK, GROUP * BLOCK), F32)]),
        out_shape=jax.ShapeDtypeStruct((n, ATTN_WIDTH), BF16),
        compiler_params=pltpu.CompilerParams(
            dimension_semantics=("arbitrary",), vmem_limit_bytes=VMEM_LIMIT),
        name="attn",
    )(sinks, q, k, k, k, v, v, v)


def _merge_kernel(x_ref, a_ref, pc_ref, pp_ref, pn_ref, ga_ref, gp_ref,
                  wab_ref, wpg_ref, ps_ref, wpb_ref, wo_ref, g1_ref, b1_ref, wr_ref, br_ref,
                  h_ref, idx_ref, gate_ref, rank_ref, cnt_ref,
                  ext_ref, carry_ref, earlier_ref, *, seq, tm, tc):
    i = pl.program_id(0)
    tiles_per_seq = seq // tm
    jt = i % tiles_per_seq

    @pl.when(i == 0)
    def _():
        carry_ref[...] = jnp.zeros_like(carry_ref)
        r_i = lax.broadcasted_iota(jnp.int32, (tc, tc), 0)
        c_i = lax.broadcasted_iota(jnp.int32, (tc, tc), 1)
        earlier_ref[...] = (r_i < c_i).astype(BF16)

    ext_ref[0:POOL_HALO, :] = jnp.where(jt > 0, pp_ref[...], 0.0)
    ext_ref[POOL_HALO:POOL_HALO + tm, :] = pc_ref[...]
    ext_ref[POOL_HALO + tm:, :] = jnp.where(jt < tiles_per_seq - 1, pn_ref[...], 0.0)

    w_hi = wr_ref[...].astype(BF16)
    w_lo = (wr_ref[...] - w_hi.astype(F32)).astype(BF16)
    w_split = jnp.concatenate([w_hi, w_lo], axis=0)
    carry = carry_ref[...]

    for r0 in range(0, tm, tc):
        rs = slice(r0, r0 + tc)
        cl = slice(r0, r0 + tc)

        spos = jt * tm + r0 + lax.broadcasted_iota(jnp.int32, (tc, 1), 0)
        pooled_parts = []
        for g, w in enumerate(POOL_WINDOWS):
            cs = slice(g * POOL_GROUP_WIDTH, (g + 1) * POOL_GROUP_WIDTH)
            acc = None
            for off in range(-(w // 2), w // 2):
                lo_row = POOL_HALO + r0 + off
                t = ext_ref[lo_row:lo_row + tc, cs]
                acc = t if acc is None else acc + t
            hi = jnp.minimum(spos + (w // 2 - 1), seq - 1)
            lo = jnp.maximum(spos - w // 2, 0)
            cnt = (hi - lo + 1).astype(F32)
            y = acc / cnt - pc_ref[rs, cs]
            yg = jnp.dot(y.astype(BF16), wpg_ref[g], preferred_element_type=F32)
            pooled_parts.append(yg * ps_ref[:, cs])
        pool_feat = jnp.concatenate(pooled_parts, axis=-1).astype(BF16)

        a_br = jnp.dot(a_ref[rs, :], wab_ref[...], preferred_element_type=F32)
        p_br = jnp.dot(pool_feat, wpb_ref[...], preferred_element_type=F32)
        mixed = ga_ref[rs, :] * a_br + gp_ref[rs, :] * p_br
        o = jnp.dot(mixed.astype(BF16), wo_ref[...], preferred_element_type=F32)
        h = _layer_norm(DEEPNORM_ALPHA * x_ref[rs, :] + o, g1_ref[...], b1_ref[...])
        for c in range(SUBLANES):
            h_ref[pl.ds(r0 * SUBLANES + c, tc, stride=SUBLANES), :] = h[:, c * LANES:(c + 1) * LANES]

        h_hi = h.astype(BF16)
        h_lo = (h - h_hi.astype(F32)).astype(BF16)
        nt = (((1,), (1,)), ((), ()))
        part = lax.dot_general(w_split, h_hi, nt, preferred_element_type=F32)
        logits = (part[:N_EXPERTS] + part[N_EXPERTS:]
                  + lax.dot_general(w_hi, h_lo, nt, preferred_element_type=F32)
                  + br_ref[...])

        eidx = lax.broadcasted_iota(jnp.int32, (N_EXPERTS, tc), 0)
        work = logits
        sel_idx, sel_val, sel_hit = [], [], []
        chosen = jnp.zeros((N_EXPERTS, tc), F32)
        for _ in range(TOP_K):
            m = jnp.max(work, axis=0, keepdims=True)
            ix = jnp.min(jnp.where(work == m, eidx, N_EXPERTS), axis=0, keepdims=True)
            hit = eidx == ix
            sel_idx.append(ix)
            sel_val.append(m)
            sel_hit.append(hit)
            chosen = jnp.where(hit, 1.0, chosen)
            work = jnp.where(hit, -jnp.inf, work)
        ex = [jnp.exp(v - sel_val[0]) for v in sel_val]
        tot = ex[0] + ex[1] + ex[2] + ex[3]
        gate_ref[:, cl] = jnp.concatenate([e / tot for e in ex], axis=0)
        idx_ref[:, cl] = jnp.concatenate(sel_idx, axis=0)

        before = jnp.dot(chosen.astype(BF16), earlier_ref[...], preferred_element_type=F32) + carry
        ranks = [jnp.sum(jnp.where(hit, before, 0.0), axis=0, keepdims=True) for hit in sel_hit]
        rank_ref[:, cl] = jnp.concatenate(ranks, axis=0).astype(jnp.int32)
        carry = carry + jnp.sum(chosen, axis=1, keepdims=True)

    carry_ref[...] = carry
    cnt_ref[...] = carry.astype(jnp.int32)


def _merge(x2, attn, p_in, g_attn, g_pool, wab, wpg, ps, wpb, wo, g1, b1, wr, br, seq):
    n, d = x2.shape
    tm = TM_MERGE
    halo_blocks = tm // POOL_HALO
    row = lambda i: (i, 0)
    col = lambda i: (0, i)
    full2 = lambda i: (0, 0)

    def prev_halo(i):
        return (jnp.maximum(i * halo_blocks - 1, 0), 0)

    def next_halo(i):
        return (jnp.minimum((i + 1) * halo_blocks, n // POOL_HALO - 1), 0)

    return pl.pallas_call(
        functools.partial(_merge_kernel, seq=seq, tm=tm, tc=TC_MERGE),
        grid=(n // tm,),
        in_specs=[pl.BlockSpec((tm, d), row),
                  pl.BlockSpec((tm, ATTN_WIDTH), row),
                  pl.BlockSpec((tm, POOL_WIDTH), row),
                  pl.BlockSpec((POOL_HALO, POOL_WIDTH), prev_halo),
                  pl.BlockSpec((POOL_HALO, POOL_WIDTH), next_halo),
                  pl.BlockSpec((tm, d), row),
                  pl.BlockSpec((tm, d), row),
                  pl.BlockSpec(wab.shape, full2),
                  pl.BlockSpec(wpg.shape, lambda i: (0, 0, 0)),
                  pl.BlockSpec(ps.shape, full2),
                  pl.BlockSpec(wpb.shape, full2),
                  pl.BlockSpec(wo.shape, full2),
                  pl.BlockSpec(g1.shape, full2),
                  pl.BlockSpec(b1.shape, full2),
                  pl.BlockSpec(wr.shape, full2),
                  pl.BlockSpec(br.shape, full2)],
        out_specs=[pl.BlockSpec((tm * SUBLANES, LANES), row),
                   pl.BlockSpec((TOP_K, tm), col),
                   pl.BlockSpec((TOP_K, tm), col),
                   pl.BlockSpec((TOP_K, tm), col),
                   pl.BlockSpec((N_EXPERTS, 1), full2)],
        out_shape=[jax.ShapeDtypeStruct((n * SUBLANES, LANES), F32),
                   jax.ShapeDtypeStruct((TOP_K, n), jnp.int32),
                   jax.ShapeDtypeStruct((TOP_K, n), F32),
                   jax.ShapeDtypeStruct((TOP_K, n), jnp.int32),
                   jax.ShapeDtypeStruct((N_EXPERTS, 1), jnp.int32)],
        scratch_shapes=[pltpu.VMEM((tm + 2 * POOL_HALO, POOL_WIDTH), F32),
                        pltpu.VMEM((N_EXPERTS, 1), F32),
                        pltpu.VMEM((TC_MERGE, TC_MERGE), BF16)],
        compiler_params=pltpu.CompilerParams(
            dimension_semantics=("arbitrary",), vmem_limit_bytes=VMEM_LIMIT),
        name="merge",
    )(x2, attn, p_in, p_in, p_in, g_attn, g_pool, wab, wpg, ps, wpb, wo, g1, b1, wr, br)


def _dispatch_kernel(pos_ref, pend_ref, nu_ref, h_ref, xs_ref, zero_ref, sem, zsem, *, tm, bm, nblk):
    base = pl.program_id(0) * (tm * TOP_K)

    @pl.when(pl.program_id(0) == 0)
    def _():
        zero_ref[...] = jnp.zeros_like(zero_ref)

        def fill(start):
            return pltpu.make_async_copy(zero_ref, xs_ref.at[pl.ds(start, bm)], zsem)

        def group_tail(e):
            prev_end = jnp.where(e > 0, pend_ref[jnp.maximum(e - 1, 0)], 0)
            return pend_ref[e] > prev_end, pl.multiple_of(pend_ref[e] - bm, bm)

        def per_expert(op):
            def body(e, c):
                nonempty, start = group_tail(e)

                @pl.when(nonempty)
                def _():
                    op(fill(start))
                return c
            lax.fori_loop(0, N_EXPERTS, body, 0)

        def per_tail_block(op):
            def body(b, c):
                op(fill(pl.multiple_of(b * bm, bm)))
                return c
            lax.fori_loop(nu_ref[0], nblk, body, 0)

        per_expert(lambda c: c.start())
        per_tail_block(lambda c: c.start())
        per_expert(lambda c: c.wait())
        per_tail_block(lambda c: c.wait())

    def row_copy(t, k):
        dst = pos_ref[base + t * TOP_K + k]
        return pltpu.make_async_copy(h_ref.at[t], xs_ref.at[dst], sem)

    def issue(tt, c):
        for u in range(ROW_UNROLL):
            for k in range(TOP_K):
                row_copy(tt * ROW_UNROLL + u, k).start(priority=k % 2)
        return c

    lax.fori_loop(0, tm // ROW_UNROLL, issue, 0)

    for _ in range(TOP_K):
        pltpu.make_async_copy(h_ref, xs_ref.at[pl.ds(0, tm)], sem).wait()


def _dispatch(pos_flat, pend, n_used, h_tiles, n_rows):
    n = h_tiles.shape[0]
    tm = TM_ROWS
    bm = BM_FFN
    return pl.pallas_call(
        functools.partial(_dispatch_kernel, tm=tm, bm=bm, nblk=n_rows // bm),
        grid_spec=pltpu.PrefetchScalarGridSpec(
            num_scalar_prefetch=3,
            grid=(n // tm,),
            in_specs=[pl.BlockSpec((tm, SUBLANES, LANES), lambda i, p, e, u: (i, 0, 0))],
            out_specs=pl.BlockSpec(memory_space=pl.ANY),
            scratch_shapes=[pltpu.VMEM((bm, SUBLANES, LANES), F32),
                            pltpu.SemaphoreType.DMA,
                            pltpu.SemaphoreType.DMA]),
        out_shape=jax.ShapeDtypeStruct((n_rows, SUBLANES, LANES), F32),
        compiler_params=pltpu.CompilerParams(
            dimension_semantics=("arbitrary",), vmem_limit_bytes=VMEM_LIMIT),
        name="dispatch",
    )(pos_flat, pend, n_used, h_tiles)


def _ffn_kernel(be_ref, nv_ref, first_ref, slot_ref, nxt_ref, nu_ref,
                x_ref, w1_hbm, b1_ref, w2_hbm, b2_ref, y_ref,
                w1f_ref, w2f_ref, w1b_ref, w2b_ref, wsem, *, bm, f):
    b = pl.program_id(0)
    used = b < nu_ref[0]

    def fetch(e, s):
        return (pltpu.make_async_copy(w1_hbm.at[e], w1f_ref.at[s], wsem.at[0, s]),
                pltpu.make_async_copy(w2_hbm.at[e], w2f_ref.at[s], wsem.at[1, s]))

    @pl.when(b == 0)
    def _():
        for c in fetch(be_ref[0], 0):
            c.start()

    @pl.when(used & (first_ref[b] == 1))
    def _():
        s = slot_ref[b]
        for c in fetch(be_ref[b], s):
            c.wait()

        @pl.when(nxt_ref[b] >= 0)
        def _():
            for c in fetch(nxt_ref[b], 1 - s):
                c.start()

        w1b_ref[...] = w1f_ref[s].astype(BF16)
        w2b_ref[...] = w2f_ref[s].astype(BF16)

    @pl.when(used)
    def _():
        rows = lax.broadcasted_iota(jnp.int32, (bm, 1), 0)
        x = jnp.where(rows < nv_ref[b], _load_row_tiles(x_ref, bm), 0.0).astype(BF16)
        h = jnp.dot(x, w1b_ref[...], preferred_element_type=F32) + b1_ref[0]
        gate = jnp.minimum(h[:, :f], SWIGLU_LIMIT)
        up = jnp.clip(h[:, f:], -SWIGLU_LIMIT, SWIGLU_LIMIT)
        act = (up + 1.0) * gate * _sigmoid(SWIGLU_ALPHA * gate)
        y = jnp.dot(act.astype(BF16), w2b_ref[...], preferred_element_type=F32) + b2_ref[0]
        _store_row_tiles(y_ref, y)

    @pl.when(b >= nu_ref[0])
    def _():
        y_ref[...] = jnp.zeros_like(y_ref)


def _ffn(blk_expert, blk_valid, blk_first, blk_slot, blk_next, n_used, xs_tiles, w1, b1, w2, b2):
    n_rows = xs_tiles.shape[0] // SUBLANES
    bm = BM_FFN
    d, f = w1.shape[1], w2.shape[1]

    def row_map(b, be, nv, fi, sl, nx, nu):
        return (jnp.minimum(b, nu[0] - 1), 0)

    def exp_map(b, be, nv, fi, sl, nx, nu):
        return (be[b], 0, 0)

    return pl.pallas_call(
        functools.partial(_ffn_kernel, bm=bm, f=f),
        grid_spec=pltpu.PrefetchScalarGridSpec(
            num_scalar_prefetch=6,
            grid=(n_rows // bm,),
            in_specs=[pl.BlockSpec((bm * SUBLANES, LANES), row_map),
                      pl.BlockSpec(memory_space=pl.ANY),
                      pl.BlockSpec((1, 1, 2 * f), exp_map),
                      pl.BlockSpec(memory_space=pl.ANY),
                      pl.BlockSpec((1, 1, d), exp_map)],
            out_specs=pl.BlockSpec((bm * SUBLANES, LANES), lambda b, *_: (b, 0)),
            scratch_shapes=[pltpu.VMEM((2, d, 2 * f), F32),
                            pltpu.VMEM((2, f, d), F32),
                            pltpu.VMEM((d, 2 * f), BF16),
                            pltpu.VMEM((f, d), BF16),
                            pltpu.SemaphoreType.DMA((2, 2))]),
        out_shape=jax.ShapeDtypeStruct((n_rows * SUBLANES, LANES), F32),
        compiler_params=pltpu.CompilerParams(
            dimension_semantics=("arbitrary",), vmem_limit_bytes=VMEM_LIMIT),
        name="ffn",
    )(blk_expert, blk_valid, blk_first, blk_slot, blk_next, n_used, xs_tiles, w1, b1, w2, b2)


def _combine_kernel(pos_ref, h_ref, gate_ref, g2_ref, b2_ref, ys_ref, o_ref, buf_ref, sem, *, tm, nsteps):
    i = pl.program_id(0)
    slot = i % 2

    def request_tile(step, s):
        base = step * (tm * TOP_K)

        def row_copy(t, k):
            src = pos_ref[base + t * TOP_K + k]
            dst = buf_ref.at[s, k, pl.ds(pl.multiple_of(t * SUBLANES, SUBLANES), SUBLANES), :]
            return pltpu.make_async_copy(ys_ref.at[src], dst, sem.at[s])

        def issue(tt, c):
            for u in range(ROW_UNROLL):
                for k in range(TOP_K):
                    row_copy(tt * ROW_UNROLL + u, k).start(priority=k % 2)
            return c

        lax.fori_loop(0, tm // ROW_UNROLL, issue, 0)

    @pl.when(i == 0)
    def _():
        request_tile(0, 0)

    for k in range(TOP_K):
        pltpu.make_async_copy(h_ref, buf_ref.at[slot, k], sem.at[slot]).wait()

    @pl.when(i + 1 < nsteps)
    def _():
        request_tile(i + 1, 1 - slot)

    f = gate_ref[:, 0:1] * _load_row_tiles(buf_ref.at[slot, 0], tm)
    for k in range(1, TOP_K):
        f = f + gate_ref[:, k:k + 1] * _load_row_tiles(buf_ref.at[slot, k], tm)
    h = _load_row_tiles(h_ref, tm)
    o_ref[...] = _layer_norm(DEEPNORM_ALPHA * h + f, g2_ref[...], b2_ref[...])


def _combine(pos_flat, h_tiles, gates, g2, b2, ys_tiles):
    n = h_tiles.shape[0] // SUBLANES
    d = SUBLANES * LANES
    tm = TM_ROWS
    row = lambda i, p: (i, 0)
    full2 = lambda i, p: (0, 0)
    return pl.pallas_call(
        functools.partial(_combine_kernel, tm=tm, nsteps=n // tm),
        grid_spec=pltpu.PrefetchScalarGridSpec(
            num_scalar_prefetch=1,
            grid=(n // tm,),
            in_specs=[pl.BlockSpec((tm * SUBLANES, LANES), row),
                      pl.BlockSpec((tm, TOP_K), row),
                      pl.BlockSpec(g2.shape, full2),
                      pl.BlockSpec(b2.shape, full2),
                      pl.BlockSpec(memory_space=pl.ANY)],
            out_specs=pl.BlockSpec((tm, d), row),
            scratch_shapes=[pltpu.VMEM((2, TOP_K, tm * SUBLANES, LANES), F32),
                            pltpu.SemaphoreType.DMA((2,))]),
        out_shape=jax.ShapeDtypeStruct((n, d), F32),
        compiler_params=pltpu.CompilerParams(
            dimension_semantics=("arbitrary",), vmem_limit_bytes=VMEM_LIMIT),
        name="combine",
    )(pos_flat, h_tiles, gates, g2, b2, ys_tiles)


def _layer(h2, seq, w_in, sinks, wab, wpg, ps, wpb, wo, g1, b1, wr, br, w1, bb1, w2, bb2, g2, b2):
    n, d = h2.shape
    q, k, v, p_in, g_attn, g_pool = _inproj(h2, w_in.astype(BF16))
    attn = _attention(q, k, v, sinks, seq)
    h1, idx, gates, rank, counts = _merge(
        h2, attn, p_in, g_attn, g_pool,
        wab.astype(BF16), wpg.astype(BF16), ps.reshape(1, -1), wpb.astype(BF16), wo.astype(BF16),
        g1.reshape(1, -1), b1.reshape(1, -1), wr.T, br.reshape(-1, 1), seq)
    idx, gates, rank = idx.T, gates.T, rank.T

    bm = BM_FFN
    counts = counts.reshape(N_EXPERTS)
    experts = jnp.arange(N_EXPERTS, dtype=jnp.int32)
    upto = experts[None, :] <= experts[:, None]

    def prefix_sum(v):
        return jnp.sum(jnp.where(upto, v[None, :], 0), axis=1)

    padded = ((counts + bm - 1) // bm) * bm
    pend = prefix_sum(padded)
    pstart = pend - padded
    n_rows = n * TOP_K + N_EXPERTS * bm
    nblk = n_rows // bm
    blk_lo = jnp.arange(nblk, dtype=jnp.int32) * bm
    blk_expert = jnp.minimum(
        jnp.sum((blk_lo[:, None] >= pend[None, :]).astype(jnp.int32), axis=1), N_EXPERTS - 1)
    blk_is = blk_expert[:, None] == experts[None, :]

    def per_block(table):
        return jnp.sum(jnp.where(blk_is, table[None, :], 0), axis=1).astype(jnp.int32)

    blk_valid = jnp.clip(per_block(pstart + counts) - blk_lo, 0, bm).astype(jnp.int32)
    n_used = (pend[-1:] // bm).astype(jnp.int32)
    nonempty = counts > 0
    group_slot = (prefix_sum(nonempty.astype(jnp.int32)) - 1) % 2
    later = jnp.where(nonempty[None, :] & ~upto, experts[None, :], N_EXPERTS)
    next_group = jnp.min(later, axis=1)
    next_group = jnp.where(next_group == N_EXPERTS, -1, next_group)
    blk_first = (blk_lo == per_block(pstart)).astype(jnp.int32)
    blk_slot = per_block(group_slot)
    blk_next = per_block(next_group)

    onehot = idx[:, :, None] == experts[None, None, :]
    pos = rank + jnp.sum(jnp.where(onehot, pstart[None, None, :], 0), axis=-1)
    pos_flat = pos.reshape(n * TOP_K).astype(jnp.int32)

    xs = _dispatch(pos_flat, pend.astype(jnp.int32), n_used,
                   h1.reshape(n, SUBLANES, LANES), n_rows)
    ys = _ffn(blk_expert.astype(jnp.int32), blk_valid, blk_first, blk_slot, blk_next, n_used,
              xs.reshape(n_rows * SUBLANES, LANES),
              w1, bb1.reshape(N_EXPERTS, 1, -1), w2, bb2.reshape(N_EXPERTS, 1, -1))
    return _combine(pos_flat, h1, gates, g2.reshape(1, -1), b2.reshape(1, -1),
                    ys.reshape(n_rows, SUBLANES, LANES))


def kernel(x, w_in, attn_sinks, w_attn_branch, w_pool_group, pool_scale, w_pool_branch, w_out,
           ln1_g, ln1_b, w_router, b_router, w_mlp1, b_mlp1, w_mlp2, b_mlp2, ln2_g, ln2_b):
    bsz, seq, d = x.shape
    assert d == SUBLANES * LANES, "row-tile layout needs one (8, 128) f32 tile per token row"
    h = x.reshape(bsz * seq, d)
    for l in range(w_in.shape[0]):
        h = _layer(h, seq, w_in[l], attn_sinks[l], w_attn_branch[l], w_pool_group[l], pool_scale[l],
                   w_pool_branch[l], w_out[l], ln1_g[l], ln1_b[l], w_router[l], b_router[l],
                   w_mlp1[l], b_mlp1[l], w_mlp2[l], b_mlp2[l], ln2_g[l], ln2_b[l])
    return h.reshape(bsz, seq, d)
```

```python
import functools
import math

import jax
import jax.numpy as jnp
import numpy as np
from jax import lax
from jax.experimental import pallas as pl
from jax.experimental.pallas import tpu as pltpu

F32 = jnp.float32
BF16 = jnp.bfloat16

N_Q_HEADS = 8
N_KV_HEADS = 2
HEAD_DIM = 64
GROUP = N_Q_HEADS // N_KV_HEADS
ATTN_WIDTH = N_Q_HEADS * HEAD_DIM
KV_WIDTH = N_KV_HEADS * HEAD_DIM
WINDOW = 128
BLOCK = 128
POOL_WINDOWS = (2, 4, 8, 16)
POOL_GROUP_WIDTH = 128
POOL_WIDTH = len(POOL_WINDOWS) * POOL_GROUP_WIDTH
POOL_HALO = 8
N_EXPERTS = 32
TOP_K = 4
SWIGLU_LIMIT = 7.0
SWIGLU_ALPHA = 1.702
LN_EPS = 1e-5
DEPTH = 1
DEEPNORM_ALPHA = (2.0 * DEPTH) ** 0.25
NEG_BIG = -1e30

TM_PROJ = 512
TM_MERGE = 512
TC_MERGE = 512
SUBLANES = 8
LANES = 128
TM_ROWS = 512
ROW_UNROLL = 2
COMBINE_CHUNK = 128
BM_FFN = 512
ZERO_CHUNK = 128
VMEM_LIMIT = 56 * 1024 * 1024


def _sigmoid(x):
    return 1.0 / (1.0 + jnp.exp(-x))


def _store_row_tiles(ref, value):
    rows = value.shape[0]
    for c in range(SUBLANES):
        ref[pl.ds(c, rows, stride=SUBLANES), :] = value[:, c * LANES:(c + 1) * LANES]


def _load_row_tiles(ref, rows):
    return jnp.concatenate(
        [ref[pl.ds(c, rows, stride=SUBLANES), :] for c in range(SUBLANES)], axis=-1)


def _layer_norm(x, g, b):
    mean = jnp.mean(x, axis=-1, keepdims=True)
    xc = x - mean
    var = jnp.mean(xc * xc, axis=-1, keepdims=True)
    return xc * lax.rsqrt(var + LN_EPS) * g + b


def _inproj_kernel(x_ref, w_ref, q_ref, k_ref, v_ref, p_ref, ga_ref, gp_ref, *, d_model):
    xb = x_ref[...].astype(BF16)
    o_k = ATTN_WIDTH
    o_v = o_k + KV_WIDTH
    o_p = o_v + KV_WIDTH
    o_g = o_p + POOL_WIDTH

    def proj(lo, hi):
        return jnp.dot(xb, w_ref[:, lo:hi], preferred_element_type=F32)

    q_ref[...] = (proj(0, o_k) * (1.0 / math.sqrt(HEAD_DIM))).astype(BF16)
    k_ref[...] = proj(o_k, o_v).astype(BF16)
    v_ref[...] = proj(o_v, o_p).astype(BF16)
    p_ref[...] = proj(o_p, o_g)
    ga_ref[...] = _sigmoid(proj(o_g, o_g + d_model))
    gp_ref[...] = _sigmoid(proj(o_g + d_model, o_g + 2 * d_model))


def _inproj(x2, w_in_b):
    n, d = x2.shape
    in_width = w_in_b.shape[1]
    tm = TM_PROJ
    row = lambda i: (i, 0)
    return pl.pallas_call(
        functools.partial(_inproj_kernel, d_model=d),
        grid=(n // tm,),
        in_specs=[pl.BlockSpec((tm, d), row),
                  pl.BlockSpec((d, in_width), lambda i: (0, 0))],
        out_specs=[pl.BlockSpec((tm, ATTN_WIDTH), row),
                   pl.BlockSpec((tm, KV_WIDTH), row),
                   pl.BlockSpec((tm, KV_WIDTH), row),
                   pl.BlockSpec((tm, POOL_WIDTH), row),
                   pl.BlockSpec((tm, d), row),
                   pl.BlockSpec((tm, d), row)],
        out_shape=[jax.ShapeDtypeStruct((n, ATTN_WIDTH), BF16),
                   jax.ShapeDtypeStruct((n, KV_WIDTH), BF16),
                   jax.ShapeDtypeStruct((n, KV_WIDTH), BF16),
                   jax.ShapeDtypeStruct((n, POOL_WIDTH), F32),
                   jax.ShapeDtypeStruct((n, d), F32),
                   jax.ShapeDtypeStruct((n, d), F32)],
        compiler_params=pltpu.CompilerParams(
            dimension_semantics=("arbitrary",), vmem_limit_bytes=VMEM_LIMIT),
        name="inproj",
    )(x2, w_in_b)


def _alibi_slope(h):
    return 2.0 ** (-8.0 * (h + 1) / N_Q_HEADS)


def _attn_kernel(sink_ref, q_ref, kp_ref, kc_ref, kn_ref, vp_ref, vc_ref, vn_ref, o_ref,
                 bias_ref, *, nb):
    j = pl.program_id(0) % nb
    cols = GROUP * BLOCK
    keys = 3 * BLOCK

    @pl.when(pl.program_id(0) == 0)
    def _():
        ki = lax.broadcasted_iota(jnp.int32, (keys, cols), 0)
        ci = lax.broadcasted_iota(jnp.int32, (keys, cols), 1)
        arel = jnp.abs(ki - BLOCK - ci % BLOCK)
        arel_f = arel.astype(F32)
        grp = ci // BLOCK
        inside = [(arel <= WINDOW) & (ki >= BLOCK), arel <= WINDOW, (arel <= WINDOW) & (ki < 2 * BLOCK)]
        for kvh in range(N_KV_HEADS):
            slope = jnp.full((keys, cols), _alibi_slope(kvh * GROUP + GROUP - 1), F32)
            for g in range(GROUP - 2, -1, -1):
                slope = jnp.where(grp == g, _alibi_slope(kvh * GROUP + g), slope)
            for edge in range(3):
                bias_ref[edge, kvh] = jnp.where(inside[edge], -(slope * arel_f), NEG_BIG)

    edge = jnp.where(j == 0, 0, jnp.where(j == nb - 1, 2, 1))
    kk = jnp.concatenate([kp_ref[...], kc_ref[...], kn_ref[...]], axis=0)
    vv = jnp.concatenate([vp_ref[...], vc_ref[...], vn_ref[...]], axis=0)
    grp_row = lax.broadcasted_iota(jnp.int32, (1, cols), 1) // BLOCK

    for kvh in range(N_KV_HEADS):
        heads = [kvh * GROUP + g for g in range(GROUP)]
        sink = jnp.full((1, cols), sink_ref[heads[GROUP - 1]], F32)
        for g in range(GROUP - 2, -1, -1):
            sink = jnp.where(grp_row == g, sink_ref[heads[g]], sink)
        qs = jnp.concatenate([q_ref[:, h * HEAD_DIM:(h + 1) * HEAD_DIM] for h in heads], axis=0)
        kh = kk[:, kvh * HEAD_DIM:(kvh + 1) * HEAD_DIM]
        vh = vv[:, kvh * HEAD_DIM:(kvh + 1) * HEAD_DIM]
        s = lax.dot_general(kh, qs, (((1,), (1,)), ((), ())), preferred_element_type=F32)
        s = s + bias_ref[edge, kvh]
        m = jnp.maximum(jnp.max(s, axis=0, keepdims=True), sink)
        p = jnp.exp(s - m)
        denom = jnp.sum(p, axis=0, keepdims=True) + jnp.exp(sink - m)
        ot = lax.dot_general(vh, p.astype(BF16), (((0,), (0,)), ((), ())),
                             preferred_element_type=F32) / denom
        o = ot.T.astype(BF16)
        for g, h in enumerate(heads):
            o_ref[:, h * HEAD_DIM:(h + 1) * HEAD_DIM] = o[g * BLOCK:(g + 1) * BLOCK]


def _attention(q, k, v, sinks, seq):
    n = q.shape[0]
    nb = seq // BLOCK
    nblk = n // BLOCK

    def prev_map(i, s):
        return (jnp.where(i % nb == 0, i, i - 1), 0)

    def next_map(i, s):
        return (jnp.where(i % nb == nb - 1, i, i + 1), 0)

    cur_map = lambda i, s: (i, 0)
    kv_spec = lambda m: pl.BlockSpec((BLOCK, KV_WIDTH), m)
    return pl.pallas_call(
        functools.partial(_attn_kernel, nb=nb),
        grid_spec=pltpu.PrefetchScalarGridSpec(
            num_scalar_prefetch=1,
            grid=(nblk,),
            in_specs=[pl.BlockSpec((BLOCK, ATTN_WIDTH), cur_map),
                      kv_spec(prev_map), kv_spec(cur_map), kv_spec(next_map),
                      kv_spec(prev_map), kv_spec(cur_map), kv_spec(next_map)],
            out_specs=pl.BlockSpec((BLOCK, ATTN_WIDTH), cur_map),
            scratch_shapes=[pltpu.VMEM((3, N_KV_HEADS, 3 * BLOCK, GROUP * BLOCK), F32)]),
        out_shape=jax.ShapeDtypeStruct((n, ATTN_WIDTH), BF16),
        compiler_params=pltpu.CompilerParams(
            dimension_semantics=("arbitrary",), vmem_limit_bytes=VMEM_LIMIT),
        name="attn",
    )(sinks, q, k, k, k, v, v, v)


def _merge_kernel(x_ref, a_ref, pc_ref, pp_ref, pn_ref, ga_ref, gp_ref,
                  wab_ref, wpg_ref, ps_ref, wpb_ref, wo_ref, g1_ref, b1_ref, wr_ref, br_ref,
                  h_ref, idx_ref, gate_ref, rank_ref, cnt_ref,
                  ext_ref, carry_ref, earlier_ref, *, seq, tm, tc):
    i = pl.program_id(0)
    tiles_per_seq = seq // tm
    jt = i % tiles_per_seq

    @pl.when(i == 0)
    def _():
        carry_ref[...] = jnp.zeros_like(carry_ref)
        r_i = lax.broadcasted_iota(jnp.int32, (tc, tc), 0)
        c_i = lax.broadcasted_iota(jnp.int32, (tc, tc), 1)
        earlier_ref[...] = (r_i < c_i).astype(BF16)

    ext_ref[0:POOL_HALO, :] = jnp.where(jt > 0, pp_ref[...], 0.0)
    ext_ref[POOL_HALO:POOL_HALO + tm, :] = pc_ref[...]
    ext_ref[POOL_HALO + tm:, :] = jnp.where(jt < tiles_per_seq - 1, pn_ref[...], 0.0)

    w_hi = wr_ref[...].astype(BF16)
    w_lo = (wr_ref[...] - w_hi.astype(F32)).astype(BF16)
    w_split = jnp.concatenate([w_hi, w_lo], axis=0)
    carry = carry_ref[...]

    for r0 in range(0, tm, tc):
        rs = slice(r0, r0 + tc)
        cl = slice(r0, r0 + tc)

        spos = jt * tm + r0 + lax.broadcasted_iota(jnp.int32, (tc, 1), 0)
        pooled_parts = []
        for g, w in enumerate(POOL_WINDOWS):
            cs = slice(g * POOL_GROUP_WIDTH, (g + 1) * POOL_GROUP_WIDTH)
            acc = None
            for off in range(-(w // 2), w // 2):
                lo_row = POOL_HALO + r0 + off
                t = ext_ref[lo_row:lo_row + tc, cs]
                acc = t if acc is None else acc + t
            hi = jnp.minimum(spos + (w // 2 - 1), seq - 1)
            lo = jnp.maximum(spos - w // 2, 0)
            cnt = (hi - lo + 1).astype(F32)
            y = acc / cnt - pc_ref[rs, cs]
            yg = jnp.dot(y.astype(BF16), wpg_ref[g], preferred_element_type=F32)
            pooled_parts.append(yg * ps_ref[:, cs])
        pool_feat = jnp.concatenate(pooled_parts, axis=-1).astype(BF16)

        a_br = jnp.dot(a_ref[rs, :], wab_ref[...], preferred_element_type=F32)
        p_br = jnp.dot(pool_feat, wpb_ref[...], preferred_element_type=F32)
        mixed = ga_ref[rs, :] * a_br + gp_ref[rs, :] * p_br
        o = jnp.dot(mixed.astype(BF16), wo_ref[...], preferred_element_type=F32)
        h = _layer_norm(DEEPNORM_ALPHA * x_ref[rs, :] + o, g1_ref[...], b1_ref[...])
        for c in range(SUBLANES):
            h_ref[pl.ds(r0 * SUBLANES + c, tc, stride=SUBLANES), :] = h[:, c * LANES:(c + 1) * LANES]

        h_hi = h.astype(BF16)
        h_lo = (h - h_hi.astype(F32)).astype(BF16)
        nt = (((1,), (1,)), ((), ()))
        part = lax.dot_general(w_split, h_hi, nt, preferred_element_type=F32)
        logits = (part[:N_EXPERTS] + part[N_EXPERTS:]
                  + lax.dot_general(w_hi, h_lo, nt, preferred_element_type=F32)
                  + br_ref[...])

        eidx = lax.broadcasted_iota(jnp.int32, (N_EXPERTS, tc), 0)
        work = logits
        sel_idx, sel_val, sel_hit = [], [], []
        chosen = jnp.zeros((N_EXPERTS, tc), F32)
        for _ in range(TOP_K):
            m = jnp.max(work, axis=0, keepdims=True)
            ix = jnp.min(jnp.where(work == m, eidx, N_EXPERTS), axis=0, keepdims=True)
            hit = eidx == ix
            sel_idx.append(ix)
            sel_val.append(m)
            sel_hit.append(hit)
            chosen = jnp.where(hit, 1.0, chosen)
            work = jnp.where(hit, -jnp.inf, work)
        ex = [jnp.exp(v - sel_val[0]) for v in sel_val]
        tot = ex[0] + ex[1] + ex[2] + ex[3]
        gate_ref[:, cl] = jnp.concatenate([e / tot for e in ex], axis=0)
        idx_ref[:, cl] = jnp.concatenate(sel_idx, axis=0)

        before = jnp.dot(chosen.astype(BF16), earlier_ref[...], preferred_element_type=F32) + carry
        ranks = [jnp.sum(jnp.where(hit, before, 0.0), axis=0, keepdims=True) for hit in sel_hit]
        rank_ref[:, cl] = jnp.concatenate(ranks, axis=0).astype(jnp.int32)
        carry = carry + jnp.sum(chosen, axis=1, keepdims=True)

    carry_ref[...] = carry
    cnt_ref[...] = carry.astype(jnp.int32)


def _merge(x2, attn, p_in, g_attn, g_pool, wab, wpg, ps, wpb, wo, g1, b1, wr, br, seq):
    n, d = x2.shape
    tm = TM_MERGE
    halo_blocks = tm // POOL_HALO
    row = lambda i: (i, 0)
    col = lambda i: (0, i)
    full2 = lambda i: (0, 0)

    def prev_halo(i):
        return (jnp.maximum(i * halo_blocks - 1, 0), 0)

    def next_halo(i):
        return (jnp.minimum((i + 1) * halo_blocks, n // POOL_HALO - 1), 0)

    return pl.pallas_call(
        functools.partial(_merge_kernel, seq=seq, tm=tm, tc=TC_MERGE),
        grid=(n // tm,),
        in_specs=[pl.BlockSpec((tm, d), row),
                  pl.BlockSpec((tm, ATTN_WIDTH), row),
                  pl.BlockSpec((tm, POOL_WIDTH), row),
                  pl.BlockSpec((POOL_HALO, POOL_WIDTH), prev_halo),
                  pl.BlockSpec((POOL_HALO, POOL_WIDTH), next_halo),
                  pl.BlockSpec((tm, d), row),
                  pl.BlockSpec((tm, d), row),
                  pl.BlockSpec(wab.shape, full2),
                  pl.BlockSpec(wpg.shape, lambda i: (0, 0, 0)),
                  pl.BlockSpec(ps.shape, full2),
                  pl.BlockSpec(wpb.shape, full2),
                  pl.BlockSpec(wo.shape, full2),
                  pl.BlockSpec(g1.shape, full2),
                  pl.BlockSpec(b1.shape, full2),
                  pl.BlockSpec(wr.shape, full2),
                  pl.BlockSpec(br.shape, full2)],
        out_specs=[pl.BlockSpec((tm * SUBLANES, LANES), row),
                   pl.BlockSpec((TOP_K, tm), col),
                   pl.BlockSpec((TOP_K, tm), col),
                   pl.BlockSpec((TOP_K, tm), col),
                   pl.BlockSpec((N_EXPERTS, 1), full2)],
        out_shape=[jax.ShapeDtypeStruct((n * SUBLANES, LANES), F32),
                   jax.ShapeDtypeStruct((TOP_K, n), jnp.int32),
                   jax.ShapeDtypeStruct((TOP_K, n), F32),
                   jax.ShapeDtypeStruct((TOP_K, n), jnp.int32),
                   jax.ShapeDtypeStruct((N_EXPERTS, 1), jnp.int32)],
        scratch_shapes=[pltpu.VMEM((tm + 2 * POOL_HALO, POOL_WIDTH), F32),
                        pltpu.VMEM((N_EXPERTS, 1), F32),
                        pltpu.VMEM((TC_MERGE, TC_MERGE), BF16)],
        compiler_params=pltpu.CompilerParams(
            dimension_semantics=("arbitrary",), vmem_limit_bytes=VMEM_LIMIT),
        name="merge",
    )(x2, attn, p_in, p_in, p_in, g_attn, g_pool, wab, wpg, ps, wpb, wo, g1, b1, wr, br)


def _dispatch_kernel(pos_ref, zfill_ref, h_ref, xs_ref, zero_ref, sem, zsem, *, tm, nchunks):
    base = pl.program_id(0) * (tm * TOP_K)

    @pl.when(pl.program_id(0) == 0)
    def _():
        zero_ref[...] = jnp.zeros_like(zero_ref)

        def flagged_chunks(op):
            def body(c, carry):
                @pl.when(zfill_ref[c] == 1)
                def _():
                    start = pl.multiple_of(c * ZERO_CHUNK, ZERO_CHUNK)
                    op(pltpu.make_async_copy(zero_ref, xs_ref.at[pl.ds(start, ZERO_CHUNK)], zsem))
                return carry
            lax.fori_loop(0, nchunks, body, 0)

        flagged_chunks(lambda c: c.start())
        flagged_chunks(lambda c: c.wait())

    def row_copy(t, k):
        dst = pos_ref[base + t * TOP_K + k]
        return pltpu.make_async_copy(h_ref.at[t], xs_ref.at[dst], sem)

    def issue(tt, c):
        for u in range(ROW_UNROLL):
            for k in range(TOP_K):
                row_copy(tt * ROW_UNROLL + u, k).start(priority=k % 2)
        return c

    lax.fori_loop(0, tm // ROW_UNROLL, issue, 0)

    for _ in range(TOP_K):
        pltpu.make_async_copy(h_ref, xs_ref.at[pl.ds(0, tm)], sem).wait()


def _dispatch(pos_flat, chunk_has_padding, h_tiles, n_rows):
    n = h_tiles.shape[0]
    tm = TM_ROWS
    return pl.pallas_call(
        functools.partial(_dispatch_kernel, tm=tm, nchunks=n_rows // ZERO_CHUNK),
        grid_spec=pltpu.PrefetchScalarGridSpec(
            num_scalar_prefetch=2,
            grid=(n // tm,),
            in_specs=[pl.BlockSpec((tm, SUBLANES, LANES), lambda i, p, z: (i, 0, 0))],
            out_specs=pl.BlockSpec(memory_space=pl.ANY),
            scratch_shapes=[pltpu.VMEM((ZERO_CHUNK, SUBLANES, LANES), F32),
                            pltpu.SemaphoreType.DMA,
                            pltpu.SemaphoreType.DMA]),
        out_shape=jax.ShapeDtypeStruct((n_rows, SUBLANES, LANES), F32),
        compiler_params=pltpu.CompilerParams(
            dimension_semantics=("arbitrary",), vmem_limit_bytes=VMEM_LIMIT),
        name="dispatch",
    )(pos_flat, chunk_has_padding, h_tiles)


def _ffn_kernel(be_ref, nv_ref, first_ref, slot_ref, nxt_ref, nu_ref,
                x_ref, w1_hbm, b1_ref, w2_hbm, b2_ref, y_ref,
                w1f_ref, w2f_ref, w1b_ref, w2b_ref, wsem, *, bm, f):
    b = pl.program_id(0)
    used = b < nu_ref[0]

    def fetch(e, s):
        return (pltpu.make_async_copy(w1_hbm.at[e], w1f_ref.at[s], wsem.at[0, s]),
                pltpu.make_async_copy(w2_hbm.at[e], w2f_ref.at[s], wsem.at[1, s]))

    @pl.when(b == 0)
    def _():
        for c in fetch(be_ref[0], 0):
            c.start()

    @pl.when(used & (first_ref[b] == 1))
    def _():
        s = slot_ref[b]
        for c in fetch(be_ref[b], s):
            c.wait()

        @pl.when(nxt_ref[b] >= 0)
        def _():
            for c in fetch(nxt_ref[b], 1 - s):
                c.start()

        w1b_ref[...] = w1f_ref[s].astype(BF16)
        w2b_ref[...] = w2f_ref[s].astype(BF16)

    def mlp(rows):
        ridx = lax.broadcasted_iota(jnp.int32, (rows, 1), 0)
        x = jnp.where(ridx < nv_ref[b], _load_row_tiles(x_ref, rows), 0.0).astype(BF16)
        h = jnp.dot(x, w1b_ref[...], preferred_element_type=F32) + b1_ref[0]
        gate = jnp.minimum(h[:, :f], SWIGLU_LIMIT)
        up = jnp.clip(h[:, f:], -SWIGLU_LIMIT, SWIGLU_LIMIT)
        act = (up + 1.0) * gate * _sigmoid(SWIGLU_ALPHA * gate)
        y = jnp.dot(act.astype(BF16), w2b_ref[...], preferred_element_type=F32) + b2_ref[0]
        _store_row_tiles(y_ref, y)

    half = bm // 2

    @pl.when(used & (nv_ref[b] > half))
    def _():
        mlp(bm)

    @pl.when(used & (nv_ref[b] <= half))
    def _():
        mlp(half)
        y_ref[half * SUBLANES:, :] = jnp.zeros((half * SUBLANES, LANES), F32)

    @pl.when(b >= nu_ref[0])
    def _():
        y_ref[...] = jnp.zeros_like(y_ref)


def _ffn(blk_expert, blk_valid, blk_first, blk_slot, blk_next, n_used, xs_tiles, w1, b1, w2, b2):
    n_rows = xs_tiles.shape[0] // SUBLANES
    bm = BM_FFN
    d, f = w1.shape[1], w2.shape[1]

    def row_map(b, be, nv, fi, sl, nx, nu):
        return (jnp.minimum(b, nu[0] - 1), 0)

    def exp_map(b, be, nv, fi, sl, nx, nu):
        return (be[b], 0, 0)

    return pl.pallas_call(
        functools.partial(_ffn_kernel, bm=bm, f=f),
        grid_spec=pltpu.PrefetchScalarGridSpec(
            num_scalar_prefetch=6,
            grid=(n_rows // bm,),
            in_specs=[pl.BlockSpec((bm * SUBLANES, LANES), row_map),
                      pl.BlockSpec(memory_space=pl.ANY),
                      pl.BlockSpec((1, 1, 2 * f), exp_map),
                      pl.BlockSpec(memory_space=pl.ANY),
                      pl.BlockSpec((1, 1, d), exp_map)],
            out_specs=pl.BlockSpec((bm * SUBLANES, LANES), lambda b, *_: (b, 0)),
            scratch_shapes=[pltpu.VMEM((2, d, 2 * f), F32),
                            pltpu.VMEM((2, f, d), F32),
                            pltpu.VMEM((d, 2 * f), BF16),
                            pltpu.VMEM((f, d), BF16),
                            pltpu.SemaphoreType.DMA((2, 2))]),
        out_shape=jax.ShapeDtypeStruct((n_rows * SUBLANES, LANES), F32),
        compiler_params=pltpu.CompilerParams(
            dimension_semantics=("arbitrary",), vmem_limit_bytes=VMEM_LIMIT),
        name="ffn",
    )(blk_expert, blk_valid, blk_first, blk_slot, blk_next, n_used, xs_tiles, w1, b1, w2, b2)


def _combine_kernel(pos_ref, h_ref, gate_ref, g2_ref, b2_ref, ys_ref, o_ref, buf_ref, sem, *, tm, nsteps):
    i = pl.program_id(0)
    slot = i % 2
    ch = COMBINE_CHUNK

    def request_rows(step, s, t0, count):
        base = step * (tm * TOP_K)
        for u in range(count):
            for k in range(TOP_K):
                src = pos_ref[base + (t0 + u) * TOP_K + k]
                row0 = pl.multiple_of((t0 + u) * SUBLANES, SUBLANES)
                dst = buf_ref.at[s, k, pl.ds(row0, SUBLANES), :]
                pltpu.make_async_copy(ys_ref.at[src], dst, sem.at[s]).start(priority=k % 2)

    def compute_rows(r0):
        def rows_of(ref):
            return jnp.concatenate(
                [ref[pl.ds(r0 * SUBLANES + c, ch, stride=SUBLANES), :] for c in range(SUBLANES)], axis=-1)

        gates = gate_ref[pl.ds(r0, ch), :]
        f = gates[:, 0:1] * rows_of(buf_ref.at[slot, 0])
        for k in range(1, TOP_K):
            f = f + gates[:, k:k + 1] * rows_of(buf_ref.at[slot, k])
        o_ref[pl.ds(r0, ch), :] = _layer_norm(DEEPNORM_ALPHA * rows_of(h_ref) + f, g2_ref[...], b2_ref[...])

    @pl.when(i == 0)
    def _():
        def issue(c, carry):
            request_rows(0, 0, c * ROW_UNROLL, ROW_UNROLL)
            return carry
        lax.fori_loop(0, tm // ROW_UNROLL, issue, 0)

    for k in range(TOP_K):
        pltpu.make_async_copy(h_ref, buf_ref.at[slot, k], sem.at[slot]).wait()

    @pl.when(i + 1 < nsteps)
    def _():
        def body(c, carry):
            r0 = pl.multiple_of(c * ch, ch)
            request_rows(i + 1, 1 - slot, r0, ch)
            compute_rows(r0)
            return carry
        lax.fori_loop(0, tm // ch, body, 0)

    @pl.when(i + 1 >= nsteps)
    def _():
        def body(c, carry):
            compute_rows(pl.multiple_of(c * ch, ch))
            return carry
        lax.fori_loop(0, tm // ch, body, 0)


def _combine(pos_flat, h_tiles, gates, g2, b2, ys_tiles):
    n = h_tiles.shape[0] // SUBLANES
    d = SUBLANES * LANES
    tm = TM_ROWS
    row = lambda i, p: (i, 0)
    full2 = lambda i, p: (0, 0)
    return pl.pallas_call(
        functools.partial(_combine_kernel, tm=tm, nsteps=n // tm),
        grid_spec=pltpu.PrefetchScalarGridSpec(
            num_scalar_prefetch=1,
            grid=(n // tm,),
            in_specs=[pl.BlockSpec((tm * SUBLANES, LANES), row),
                      pl.BlockSpec((tm, TOP_K), row),
                      pl.BlockSpec(g2.shape, full2),
                      pl.BlockSpec(b2.shape, full2),
                      pl.BlockSpec(memory_space=pl.ANY)],
            out_specs=pl.BlockSpec((tm, d), row),
            scratch_shapes=[pltpu.VMEM((2, TOP_K, tm * SUBLANES, LANES), F32),
                            pltpu.SemaphoreType.DMA((2,))]),
        out_shape=jax.ShapeDtypeStruct((n, d), F32),
        compiler_params=pltpu.CompilerParams(
            dimension_semantics=("arbitrary",), vmem_limit_bytes=VMEM_LIMIT),
        name="combine",
    )(pos_flat, h_tiles, gates, g2, b2, ys_tiles)


def _layer(h2, seq, w_in, sinks, wab, wpg, ps, wpb, wo, g1, b1, wr, br, w1, bb1, w2, bb2, g2, b2):
    n, d = h2.shape
    q, k, v, p_in, g_attn, g_pool = _inproj(h2, w_in.astype(BF16))
    attn = _attention(q, k, v, sinks, seq)
    h1, idx, gates, rank, counts = _merge(
        h2, attn, p_in, g_attn, g_pool,
        wab.astype(BF16), wpg.astype(BF16), ps.reshape(1, -1), wpb.astype(BF16), wo.astype(BF16),
        g1.reshape(1, -1), b1.reshape(1, -1), wr.T, br.reshape(-1, 1), seq)
    idx, gates, rank = idx.T, gates.T, rank.T

    bm = BM_FFN
    counts = counts.reshape(N_EXPERTS)
    experts = jnp.arange(N_EXPERTS, dtype=jnp.int32)
    upto = experts[None, :] <= experts[:, None]

    def prefix_sum(v):
        return jnp.sum(jnp.where(upto, v[None, :], 0), axis=1)

    padded = ((counts + bm - 1) // bm) * bm
    pend = prefix_sum(padded)
    pstart = pend - padded
    n_rows = n * TOP_K + N_EXPERTS * bm
    nblk = n_rows // bm
    blk_lo = jnp.arange(nblk, dtype=jnp.int32) * bm
    blk_expert = jnp.minimum(
        jnp.sum((blk_lo[:, None] >= pend[None, :]).astype(jnp.int32), axis=1), N_EXPERTS - 1)
    blk_is = blk_expert[:, None] == experts[None, :]

    def per_block(table):
        return jnp.sum(jnp.where(blk_is, table[None, :], 0), axis=1).astype(jnp.int32)

    blk_valid = jnp.clip(per_block(pstart + counts) - blk_lo, 0, bm).astype(jnp.int32)
    per_blk = bm // ZERO_CHUNK
    chunk_end_in_blk = (jnp.arange(per_blk, dtype=jnp.int32) + 1) * ZERO_CHUNK
    chunk_has_padding = (chunk_end_in_blk[None, :] > blk_valid[:, None]).astype(jnp.int32).reshape(-1)
    n_used = (pend[-1:] // bm).astype(jnp.int32)
    nonempty = counts > 0
    group_slot = (prefix_sum(nonempty.astype(jnp.int32)) - 1) % 2
    later = jnp.where(nonempty[None, :] & ~upto, experts[None, :], N_EXPERTS)
    next_group = jnp.min(later, axis=1)
    next_group = jnp.where(next_group == N_EXPERTS, -1, next_group)
    blk_first = (blk_lo == per_block(pstart)).astype(jnp.int32)
    blk_slot = per_block(group_slot)
    blk_next = per_block(next_group)

    onehot = idx[:, :, None] == experts[None, None, :]
    pos = rank + jnp.sum(jnp.where(onehot, pstart[None, None, :], 0), axis=-1)
    pos_flat = pos.reshape(n * TOP_K).astype(jnp.int32)

    xs = _dispatch(pos_flat, chunk_has_padding, h1.reshape(n, SUBLANES, LANES), n_rows)
    ys = _ffn(blk_expert.astype(jnp.int32), blk_valid, blk_first, blk_slot, blk_next, n_used,
              xs.reshape(n_rows * SUBLANES, LANES),
              w1, bb1.reshape(N_EXPERTS, 1, -1), w2, bb2.reshape(N_EXPERTS, 1, -1))
    return _combine(pos_flat, h1, gates, g2.reshape(1, -1), b2.reshape(1, -1),
                    ys.reshape(n_rows, SUBLANES, LANES))


def kernel(x, w_in, attn_sinks, w_attn_branch, w_pool_group, pool_scale, w_pool_branch, w_out,
           ln1_g, ln1_b, w_router, b_router, w_mlp1, b_mlp1, w_mlp2, b_mlp2, ln2_g, ln2_b):
    bsz, seq, d = x.shape
    assert d == SUBLANES * LANES, "row-tile layout needs one (8, 128) f32 tile per token row"
    h = x.reshape(bsz * seq, d)
    for l in range(w_in.shape[0]):
        h = _layer(h, seq, w_in[l], attn_sinks[l], w_attn_branch[l], w_pool_group[l], pool_scale[l],
                   w_pool_branch[l], w_out[l], ln1_g[l], ln1_b[l], w_router[l], b_router[l],
                   w_mlp1[l], b_mlp1[l], w_mlp2[l], b_mlp2[l], ln2_g[l], ln2_b[l])
    return h.reshape(bsz, seq, d)
```

```python
import functools
import math

import jax
import jax.numpy as jnp
import numpy as np
from jax import lax
from jax.experimental import pallas as pl
from jax.experimental.pallas import tpu as pltpu

F32 = jnp.float32
BF16 = jnp.bfloat16

N_Q_HEADS = 8
N_KV_HEADS = 2
HEAD_DIM = 64
GROUP = N_Q_HEADS // N_KV_HEADS
ATTN_WIDTH = N_Q_HEADS * HEAD_DIM
KV_WIDTH = N_KV_HEADS * HEAD_DIM
WINDOW = 128
BLOCK = 128
POOL_WINDOWS = (2, 4, 8, 16)
POOL_GROUP_WIDTH = 128
POOL_WIDTH = len(POOL_WINDOWS) * POOL_GROUP_WIDTH
POOL_HALO = 8
N_EXPERTS = 32
TOP_K = 4
SWIGLU_LIMIT = 7.0
SWIGLU_ALPHA = 1.702
LN_EPS = 1e-5
DEPTH = 1
DEEPNORM_ALPHA = (2.0 * DEPTH) ** 0.25
NEG_BIG = -1e30

TM_PROJ = 512
ATTN_QB = 2
TM_MERGE = 512
TC_MERGE = 512
SUBLANES = 8
LANES = 128
TM_ROWS = 512
ROW_UNROLL = 2
COMBINE_CHUNK = 128
BM_FFN = 512
ZERO_CHUNK = 128
VMEM_LIMIT = 56 * 1024 * 1024


def _sigmoid(x):
    return 1.0 / (1.0 + jnp.exp(-x))


def _store_row_tiles(ref, value):
    rows = value.shape[0]
    for c in range(SUBLANES):
        ref[pl.ds(c, rows, stride=SUBLANES), :] = value[:, c * LANES:(c + 1) * LANES]


def _load_row_tiles(ref, rows):
    return jnp.concatenate(
        [ref[pl.ds(c, rows, stride=SUBLANES), :] for c in range(SUBLANES)], axis=-1)


def _layer_norm(x, g, b):
    mean = jnp.mean(x, axis=-1, keepdims=True)
    xc = x - mean
    var = jnp.mean(xc * xc, axis=-1, keepdims=True)
    return xc * lax.rsqrt(var + LN_EPS) * g + b


def _inproj_kernel(x_ref, w_ref, q_ref, k_ref, v_ref, p_ref, ga_ref, gp_ref, *, d_model):
    xb = x_ref[...].astype(BF16)
    o_k = ATTN_WIDTH
    o_v = o_k + KV_WIDTH
    o_p = o_v + KV_WIDTH
    o_g = o_p + POOL_WIDTH

    def proj(lo, hi):
        return jnp.dot(xb, w_ref[:, lo:hi], preferred_element_type=F32)

    q_ref[...] = (proj(0, o_k) * (1.0 / math.sqrt(HEAD_DIM))).astype(BF16)
    k_ref[...] = proj(o_k, o_v).astype(BF16)
    v_ref[...] = proj(o_v, o_p).astype(BF16)
    p_ref[...] = proj(o_p, o_g)
    ga_ref[...] = _sigmoid(proj(o_g, o_g + d_model))
    gp_ref[...] = _sigmoid(proj(o_g + d_model, o_g + 2 * d_model))


def _inproj(x2, w_in_b):
    n, d = x2.shape
    in_width = w_in_b.shape[1]
    tm = TM_PROJ
    row = lambda i: (i, 0)
    return pl.pallas_call(
        functools.partial(_inproj_kernel, d_model=d),
        grid=(n // tm,),
        in_specs=[pl.BlockSpec((tm, d), row),
                  pl.BlockSpec((d, in_width), lambda i: (0, 0))],
        out_specs=[pl.BlockSpec((tm, ATTN_WIDTH), row),
                   pl.BlockSpec((tm, KV_WIDTH), row),
                   pl.BlockSpec((tm, KV_WIDTH), row),
                   pl.BlockSpec((tm, POOL_WIDTH), row),
                   pl.BlockSpec((tm, d), row),
                   pl.BlockSpec((tm, d), row)],
        out_shape=[jax.ShapeDtypeStruct((n, ATTN_WIDTH), BF16),
                   jax.ShapeDtypeStruct((n, KV_WIDTH), BF16),
                   jax.ShapeDtypeStruct((n, KV_WIDTH), BF16),
                   jax.ShapeDtypeStruct((n, POOL_WIDTH), F32),
                   jax.ShapeDtypeStruct((n, d), F32),
                   jax.ShapeDtypeStruct((n, d), F32)],
        compiler_params=pltpu.CompilerParams(
            dimension_semantics=("arbitrary",), vmem_limit_bytes=VMEM_LIMIT),
        name="inproj",
    )(x2, w_in_b)


def _alibi_slope(h):
    return 2.0 ** (-8.0 * (h + 1) / N_Q_HEADS)


def _attn_kernel(sink_ref, q_ref, kp_ref, kc_ref, kn_ref, vp_ref, vc_ref, vn_ref, o_ref,
                 bias_ref, *, nb):
    j = pl.program_id(0) % nb
    cols = GROUP * BLOCK
    keys = 3 * BLOCK

    @pl.when(pl.program_id(0) == 0)
    def _():
        ki = lax.broadcasted_iota(jnp.int32, (keys, cols), 0)
        ci = lax.broadcasted_iota(jnp.int32, (keys, cols), 1)
        arel = jnp.abs(ki - BLOCK - ci % BLOCK)
        arel_f = arel.astype(F32)
        grp = ci // BLOCK
        inside = [(arel <= WINDOW) & (ki >= BLOCK), arel <= WINDOW, (arel <= WINDOW) & (ki < 2 * BLOCK)]
        for kvh in range(N_KV_HEADS):
            slope = jnp.full((keys, cols), _alibi_slope(kvh * GROUP + GROUP - 1), F32)
            for g in range(GROUP - 2, -1, -1):
                slope = jnp.where(grp == g, _alibi_slope(kvh * GROUP + g), slope)
            for edge in range(3):
                bias_ref[edge, kvh] = jnp.where(inside[edge], -(slope * arel_f), NEG_BIG)

    kk = jnp.concatenate([kp_ref[...], kc_ref[...], kn_ref[...]], axis=0)
    vv = jnp.concatenate([vp_ref[...], vc_ref[...], vn_ref[...]], axis=0)
    grp_row = lax.broadcasted_iota(jnp.int32, (1, cols), 1) // BLOCK

    for qb in range(ATTN_QB):
        edge = 1
        if qb == ATTN_QB - 1:
            edge = jnp.where(j == nb - 1, 2, edge)
        if qb == 0:
            edge = jnp.where(j == 0, 0, edge)
        qrows = slice(qb * BLOCK, (qb + 1) * BLOCK)
        krows = slice(qb * BLOCK, qb * BLOCK + keys)
        for kvh in range(N_KV_HEADS):
            heads = [kvh * GROUP + g for g in range(GROUP)]
            sink = jnp.full((1, cols), sink_ref[heads[GROUP - 1]], F32)
            for g in range(GROUP - 2, -1, -1):
                sink = jnp.where(grp_row == g, sink_ref[heads[g]], sink)
            qs = jnp.concatenate(
                [q_ref[qrows, h * HEAD_DIM:(h + 1) * HEAD_DIM] for h in heads], axis=0)
            kh = kk[krows, kvh * HEAD_DIM:(kvh + 1) * HEAD_DIM]
            vh = vv[krows, kvh * HEAD_DIM:(kvh + 1) * HEAD_DIM]
            s = lax.dot_general(kh, qs, (((1,), (1,)), ((), ())), preferred_element_type=F32)
            s = s + bias_ref[edge, kvh]
            m = jnp.maximum(jnp.max(s, axis=0, keepdims=True), sink)
            p = jnp.exp(s - m)
            denom = jnp.sum(p, axis=0, keepdims=True) + jnp.exp(sink - m)
            ot = lax.dot_general(vh, p.astype(BF16), (((0,), (0,)), ((), ())),
                                 preferred_element_type=F32) / denom
            o = ot.T.astype(BF16)
            for g, h in enumerate(heads):
                o_ref[qrows, h * HEAD_DIM:(h + 1) * HEAD_DIM] = o[g * BLOCK:(g + 1) * BLOCK]


def _attention(q, k, v, sinks, seq):
    n = q.shape[0]
    tq = ATTN_QB * BLOCK
    nb = seq // tq
    assert nb > 1 and seq % tq == 0

    def prev_map(i, s):
        return (jnp.where(i % nb == 0, i * ATTN_QB, i * ATTN_QB - 1), 0)

    def next_map(i, s):
        return (jnp.where(i % nb == nb - 1, i * ATTN_QB, (i + 1) * ATTN_QB), 0)

    cur_map = lambda i, s: (i, 0)
    kv_spec = lambda m: pl.BlockSpec((BLOCK, KV_WIDTH), m)
    kv_cur = pl.BlockSpec((tq, KV_WIDTH), cur_map)
    return pl.pallas_call(
        functools.partial(_attn_kernel, nb=nb),
        grid_spec=pltpu.PrefetchScalarGridSpec(
            num_scalar_prefetch=1,
            grid=(n // tq,),
            in_specs=[pl.BlockSpec((tq, ATTN_WIDTH), cur_map),
                      kv_spec(prev_map), kv_cur, kv_spec(next_map),
                      kv_spec(prev_map), kv_cur, kv_spec(next_map)],
            out_specs=pl.BlockSpec((tq, ATTN_WIDTH), cur_map),
            scratch_shapes=[pltpu.VMEM((3, N_KV_HEADS, 3 * BLOCK, GROUP * BLOCK), F32)]),
        out_shape=jax.ShapeDtypeStruct((n, ATTN_WIDTH), BF16),
        compiler_params=pltpu.CompilerParams(
            dimension_semantics=("arbitrary",), vmem_limit_bytes=VMEM_LIMIT),
        name="attn",
    )(sinks, q, k, k, k, v, v, v)


def _merge_kernel(x_ref, a_ref, pc_ref, pp_ref, pn_ref, ga_ref, gp_ref,
                  wab_ref, wpg_ref, ps_ref, wpb_ref, wo_ref, g1_ref, b1_ref, wr_ref, br_ref,
                  h_ref, idx_ref, gate_ref, rank_ref, cnt_ref,
                  ext_ref, carry_ref, earlier_ref, *, seq, tm, tc):
    i = pl.program_id(0)
    tiles_per_seq = seq // tm
    jt = i % tiles_per_seq

    @pl.when(i == 0)
    def _():
        carry_ref[...] = jnp.zeros_like(carry_ref)
        r_i = lax.broadcasted_iota(jnp.int32, (tc, tc), 0)
        c_i = lax.broadcasted_iota(jnp.int32, (tc, tc), 1)
        earlier_ref[...] = (r_i < c_i).astype(BF16)

    ext_ref[0:POOL_HALO, :] = jnp.where(jt > 0, pp_ref[...], 0.0)
    ext_ref[POOL_HALO:POOL_HALO + tm, :] = pc_ref[...]
    ext_ref[POOL_HALO + tm:, :] = jnp.where(jt < tiles_per_seq - 1, pn_ref[...], 0.0)

    w_hi = wr_ref[...].astype(BF16)
    w_lo = (wr_ref[...] - w_hi.astype(F32)).astype(BF16)
    w_split = jnp.concatenate([w_hi, w_lo], axis=0)
    carry = carry_ref[...]

    for r0 in range(0, tm, tc):
        rs = slice(r0, r0 + tc)
        cl = slice(r0, r0 + tc)

        spos = jt * tm + r0 + lax.broadcasted_iota(jnp.int32, (tc, 1), 0)
        pooled_parts = []
        for g, w in enumerate(POOL_WINDOWS):
            cs = slice(g * POOL_GROUP_WIDTH, (g + 1) * POOL_GROUP_WIDTH)
            acc = None
            for off in range(-(w // 2), w // 2):
                lo_row = POOL_HALO + r0 + off
                t = ext_ref[lo_row:lo_row + tc, cs]
                acc = t if acc is None else acc + t
            hi = jnp.minimum(spos + (w // 2 - 1), seq - 1)
            lo = jnp.maximum(spos - w // 2, 0)
            cnt = (hi - lo + 1).astype(F32)
            y = acc / cnt - pc_ref[rs, cs]
            yg = jnp.dot(y.astype(BF16), wpg_ref[g], preferred_element_type=F32)
            pooled_parts.append(yg * ps_ref[:, cs])
        pool_feat = jnp.concatenate(pooled_parts, axis=-1).astype(BF16)

        a_br = jnp.dot(a_ref[rs, :], wab_ref[...], preferred_element_type=F32)
        p_br = jnp.dot(pool_feat, wpb_ref[...], preferred_element_type=F32)
        mixed = ga_ref[rs, :] * a_br + gp_ref[rs, :] * p_br
        o = jnp.dot(mixed.astype(BF16), wo_ref[...], preferred_element_type=F32)
        h = _layer_norm(DEEPNORM_ALPHA * x_ref[rs, :] + o, g1_ref[...], b1_ref[...])
        for c in range(SUBLANES):
            h_ref[pl.ds(r0 * SUBLANES + c, tc, stride=SUBLANES), :] = h[:, c * LANES:(c + 1) * LANES]

        h_hi = h.astype(BF16)
        h_lo = (h - h_hi.astype(F32)).astype(BF16)
        nt = (((1,), (1,)), ((), ()))
        part = lax.dot_general(w_split, h_hi, nt, preferred_element_type=F32)
        logits = (part[:N_EXPERTS] + part[N_EXPERTS:]
                  + lax.dot_general(w_hi, h_lo, nt, preferred_element_type=F32)
                  + br_ref[...])

        eidx = lax.broadcasted_iota(jnp.int32, (N_EXPERTS, tc), 0)
        work = logits
        sel_idx, sel_val, sel_hit = [], [], []
        chosen = jnp.zeros((N_EXPERTS, tc), F32)
        for _ in range(TOP_K):
            m = jnp.max(work, axis=0, keepdims=True)
            ix = jnp.min(jnp.where(work == m, eidx, N_EXPERTS), axis=0, keepdims=True)
            hit = eidx == ix
            sel_idx.append(ix)
            sel_val.append(m)
            sel_hit.append(hit)
            chosen = jnp.where(hit, 1.0, chosen)
            work = jnp.where(hit, -jnp.inf, work)
        ex = [jnp.exp(v - sel_val[0]) for v in sel_val]
        tot = ex[0] + ex[1] + ex[2] + ex[3]
        gate_ref[:, cl] = jnp.concatenate([e / tot for e in ex], axis=0)
        idx_ref[:, cl] = jnp.concatenate(sel_idx, axis=0)

        before = jnp.dot(chosen.astype(BF16), earlier_ref[...], preferred_element_type=F32) + carry
        ranks = [jnp.sum(jnp.where(hit, before, 0.0), axis=0, keepdims=True) for hit in sel_hit]
        rank_ref[:, cl] = jnp.concatenate(ranks, axis=0).astype(jnp.int32)
        carry = carry + jnp.sum(chosen, axis=1, keepdims=True)

    carry_ref[...] = carry
    cnt_ref[...] = carry.astype(jnp.int32)


def _merge(x2, attn, p_in, g_attn, g_pool, wab, wpg, ps, wpb, wo, g1, b1, wr, br, seq):
    n, d = x2.shape
    tm = TM_MERGE
    halo_blocks = tm // POOL_HALO
    row = lambda i: (i, 0)
    col = lambda i: (0, i)
    full2 = lambda i: (0, 0)

    def prev_halo(i):
        return (jnp.maximum(i * halo_blocks - 1, 0), 0)

    def next_halo(i):
        return (jnp.minimum((i + 1) * halo_blocks, n // POOL_HALO - 1), 0)

    return pl.pallas_call(
        functools.partial(_merge_kernel, seq=seq, tm=tm, tc=TC_MERGE),
        grid=(n // tm,),
        in_specs=[pl.BlockSpec((tm, d), row),
                  pl.BlockSpec((tm, ATTN_WIDTH), row),
                  pl.BlockSpec((tm, POOL_WIDTH), row),
                  pl.BlockSpec((POOL_HALO, POOL_WIDTH), prev_halo),
                  pl.BlockSpec((POOL_HALO, POOL_WIDTH), next_halo),
                  pl.BlockSpec((tm, d), row),
                  pl.BlockSpec((tm, d), row),
                  pl.BlockSpec(wab.shape, full2),
                  pl.BlockSpec(wpg.shape, lambda i: (0, 0, 0)),
                  pl.BlockSpec(ps.shape, full2),
                  pl.BlockSpec(wpb.shape, full2),
                  pl.BlockSpec(wo.shape, full2),
                  pl.BlockSpec(g1.shape, full2),
                  pl.BlockSpec(b1.shape, full2),
                  pl.BlockSpec(wr.shape, full2),
                  pl.BlockSpec(br.shape, full2)],
        out_specs=[pl.BlockSpec((tm * SUBLANES, LANES), row),
                   pl.BlockSpec((TOP_K, tm), col),
                   pl.BlockSpec((TOP_K, tm), col),
                   pl.BlockSpec((TOP_K, tm), col),
                   pl.BlockSpec((N_EXPERTS, 1), full2)],
        out_shape=[jax.ShapeDtypeStruct((n * SUBLANES, LANES), F32),
                   jax.ShapeDtypeStruct((TOP_K, n), jnp.int32),
                   jax.ShapeDtypeStruct((TOP_K, n), F32),
                   jax.ShapeDtypeStruct((TOP_K, n), jnp.int32),
                   jax.ShapeDtypeStruct((N_EXPERTS, 1), jnp.int32)],
        scratch_shapes=[pltpu.VMEM((tm + 2 * POOL_HALO, POOL_WIDTH), F32),
                        pltpu.VMEM((N_EXPERTS, 1), F32),
                        pltpu.VMEM((TC_MERGE, TC_MERGE), BF16)],
        compiler_params=pltpu.CompilerParams(
            dimension_semantics=("arbitrary",), vmem_limit_bytes=VMEM_LIMIT),
        name="merge",
    )(x2, attn, p_in, p_in, p_in, g_attn, g_pool, wab, wpg, ps, wpb, wo, g1, b1, wr, br)


def _dispatch_kernel(pos_ref, zfill_ref, h_ref, xs_ref, zero_ref, tile_ref, sem, fsem, zsem,
                     *, tm, nchunks, nsteps):
    i = pl.program_id(0)
    base = i * (tm * TOP_K)

    @pl.when(pl.program_id(0) == 0)
    def _():
        zero_ref[...] = jnp.zeros_like(zero_ref)

        def flagged_chunks(op):
            def body(c, carry):
                @pl.when(zfill_ref[c] == 1)
                def _():
                    start = pl.multiple_of(c * ZERO_CHUNK, ZERO_CHUNK)
                    op(pltpu.make_async_copy(zero_ref, xs_ref.at[pl.ds(start, ZERO_CHUNK)], zsem))
                return carry
            lax.fori_loop(0, nchunks, body, 0)

        flagged_chunks(lambda c: c.start())
        flagged_chunks(lambda c: c.wait())

    slot = i % 3
    nxt = (i + 1) % 3

    def fetch(step, s):
        src = h_ref.at[pl.ds(pl.multiple_of(step * tm, tm), tm)]
        return pltpu.make_async_copy(src, tile_ref.at[s], fsem.at[s])

    def wait_rows(s):
        for _ in range(TOP_K):
            pltpu.make_async_copy(tile_ref.at[s], xs_ref.at[pl.ds(0, tm)], sem.at[s]).wait()

    @pl.when(i == 0)
    def _():
        fetch(0, 0).start()

    @pl.when(i + 1 < nsteps)
    def _():
        @pl.when(i >= 2)
        def _():
            wait_rows(nxt)
        fetch(i + 1, nxt).start()

    fetch(i, slot).wait()

    def issue(tt, c):
        for u in range(ROW_UNROLL):
            for k in range(TOP_K):
                t = tt * ROW_UNROLL + u
                dst = pos_ref[base + t * TOP_K + k]
                pltpu.make_async_copy(tile_ref.at[slot, t], xs_ref.at[dst],
                                      sem.at[slot]).start(priority=k % 2)
        return c

    lax.fori_loop(0, tm // ROW_UNROLL, issue, 0)

    @pl.when(i == nsteps - 1)
    def _():
        for s in range(3):
            wait_rows(s)


def _dispatch(pos_flat, chunk_has_padding, h_tiles, n_rows):
    n = h_tiles.shape[0]
    tm = TM_ROWS
    assert n // tm >= 3, "the staging ring assumes at least three token tiles"
    return pl.pallas_call(
        functools.partial(_dispatch_kernel, tm=tm, nchunks=n_rows // ZERO_CHUNK, nsteps=n // tm),
        grid_spec=pltpu.PrefetchScalarGridSpec(
            num_scalar_prefetch=2,
            grid=(n // tm,),
            in_specs=[pl.BlockSpec(memory_space=pl.ANY)],
            out_specs=pl.BlockSpec(memory_space=pl.ANY),
            scratch_shapes=[pltpu.VMEM((ZERO_CHUNK, SUBLANES, LANES), F32),
                            pltpu.VMEM((3, tm, SUBLANES, LANES), F32),
                            pltpu.SemaphoreType.DMA((3,)),
                            pltpu.SemaphoreType.DMA((3,)),
                            pltpu.SemaphoreType.DMA]),
        out_shape=jax.ShapeDtypeStruct((n_rows, SUBLANES, LANES), F32),
        compiler_params=pltpu.CompilerParams(
            dimension_semantics=("arbitrary",), vmem_limit_bytes=VMEM_LIMIT),
        name="dispatch",
    )(pos_flat, chunk_has_padding, h_tiles)


def _ffn_kernel(be_ref, nv_ref, first_ref, slot_ref, nxt_ref, nu_ref,
                x_ref, w1_hbm, b1_ref, w2_hbm, b2_ref, y_ref,
                w1f_ref, w2f_ref, w1b_ref, w2b_ref, wsem, *, bm, f):
    b = pl.program_id(0)
    used = b < nu_ref[0]

    def fetch(e, s):
        return (pltpu.make_async_copy(w1_hbm.at[e], w1f_ref.at[s], wsem.at[0, s]),
                pltpu.make_async_copy(w2_hbm.at[e], w2f_ref.at[s], wsem.at[1, s]))

    @pl.when(b == 0)
    def _():
        for c in fetch(be_ref[0], 0):
            c.start()

    @pl.when(used & (first_ref[b] == 1))
    def _():
        s = slot_ref[b]
        for c in fetch(be_ref[b], s):
            c.wait()

        @pl.when(nxt_ref[b] >= 0)
        def _():
            for c in fetch(nxt_ref[b], 1 - s):
                c.start()

        w1b_ref[...] = w1f_ref[s].astype(BF16)
        w2b_ref[...] = w2f_ref[s].astype(BF16)

    def mlp(rows):
        ridx = lax.broadcasted_iota(jnp.int32, (rows, 1), 0)
        x = jnp.where(ridx < nv_ref[b], _load_row_tiles(x_ref, rows), 0.0).astype(BF16)
        h = jnp.dot(x, w1b_ref[...], preferred_element_type=F32) + b1_ref[0]
        gate = jnp.minimum(h[:, :f], SWIGLU_LIMIT)
        up = jnp.clip(h[:, f:], -SWIGLU_LIMIT, SWIGLU_LIMIT)
        act = (up + 1.0) * gate * _sigmoid(SWIGLU_ALPHA * gate)
        y = jnp.dot(act.astype(BF16), w2b_ref[...], preferred_element_type=F32) + b2_ref[0]
        _store_row_tiles(y_ref, y)

    half = bm // 2

    @pl.when(used & (nv_ref[b] > half))
    def _():
        mlp(bm)

    @pl.when(used & (nv_ref[b] <= half))
    def _():
        mlp(half)
        y_ref[half * SUBLANES:, :] = jnp.zeros((half * SUBLANES, LANES), F32)

    @pl.when(b >= nu_ref[0])
    def _():
        y_ref[...] = jnp.zeros_like(y_ref)


def _ffn(blk_expert, blk_valid, blk_first, blk_slot, blk_next, n_used, xs_tiles, w1, b1, w2, b2):
    n_rows = xs_tiles.shape[0] // SUBLANES
    bm = BM_FFN
    d, f = w1.shape[1], w2.shape[1]

    def row_map(b, be, nv, fi, sl, nx, nu):
        return (jnp.minimum(b, nu[0] - 1), 0)

    def exp_map(b, be, nv, fi, sl, nx, nu):
        return (be[b], 0, 0)

    return pl.pallas_call(
        functools.partial(_ffn_kernel, bm=bm, f=f),
        grid_spec=pltpu.PrefetchScalarGridSpec(
            num_scalar_prefetch=6,
            grid=(n_rows // bm,),
            in_specs=[pl.BlockSpec((bm * SUBLANES, LANES), row_map),
                      pl.BlockSpec(memory_space=pl.ANY),
                      pl.BlockSpec((1, 1, 2 * f), exp_map),
                      pl.BlockSpec(memory_space=pl.ANY),
                      pl.BlockSpec((1, 1, d), exp_map)],
            out_specs=pl.BlockSpec((bm * SUBLANES, LANES), lambda b, *_: (b, 0)),
            scratch_shapes=[pltpu.VMEM((2, d, 2 * f), F32),
                            pltpu.VMEM((2, f, d), F32),
                            pltpu.VMEM((d, 2 * f), BF16),
                            pltpu.VMEM((f, d), BF16),
                            pltpu.SemaphoreType.DMA((2, 2))]),
        out_shape=jax.ShapeDtypeStruct((n_rows * SUBLANES, LANES), F32),
        compiler_params=pltpu.CompilerParams(
            dimension_semantics=("arbitrary",), vmem_limit_bytes=VMEM_LIMIT),
        name="ffn",
    )(blk_expert, blk_valid, blk_first, blk_slot, blk_next, n_used, xs_tiles, w1, b1, w2, b2)


def _combine_kernel(pos_ref, h_ref, gate_ref, g2_ref, b2_ref, ys_ref, o_ref, buf_ref, sem, *, tm, nsteps):
    i = pl.program_id(0)
    slot = i % 2
    ch = COMBINE_CHUNK

    def request_rows(step, s, t0, count):
        base = step * (tm * TOP_K)
        for u in range(count):
            for k in range(TOP_K):
                src = pos_ref[base + (t0 + u) * TOP_K + k]
                row0 = pl.multiple_of((t0 + u) * SUBLANES, SUBLANES)
                dst = buf_ref.at[s, k, pl.ds(row0, SUBLANES), :]
                pltpu.make_async_copy(ys_ref.at[src], dst, sem.at[s]).start(priority=k % 2)

    def compute_rows(r0):
        def rows_of(ref):
            return jnp.concatenate(
                [ref[pl.ds(r0 * SUBLANES + c, ch, stride=SUBLANES), :] for c in range(SUBLANES)], axis=-1)

        gates = gate_ref[pl.ds(r0, ch), :]
        f = gates[:, 0:1] * rows_of(buf_ref.at[slot, 0])
        for k in range(1, TOP_K):
            f = f + gates[:, k:k + 1] * rows_of(buf_ref.at[slot, k])
        o_ref[pl.ds(r0, ch), :] = _layer_norm(DEEPNORM_ALPHA * rows_of(h_ref) + f, g2_ref[...], b2_ref[...])

    @pl.when(i == 0)
    def _():
        def issue(c, carry):
            request_rows(0, 0, c * ROW_UNROLL, ROW_UNROLL)
            return carry
        lax.fori_loop(0, tm // ROW_UNROLL, issue, 0)

    for k in range(TOP_K):
        pltpu.make_async_copy(h_ref, buf_ref.at[slot, k], sem.at[slot]).wait()

    @pl.when(i + 1 < nsteps)
    def _():
        def body(c, carry):
            r0 = pl.multiple_of(c * ch, ch)
            request_rows(i + 1, 1 - slot, r0, ch)
            compute_rows(r0)
            return carry
        lax.fori_loop(0, tm // ch, body, 0)

    @pl.when(i + 1 >= nsteps)
    def _():
        def body(c, carry):
            compute_rows(pl.multiple_of(c * ch, ch))
            return carry
        lax.fori_loop(0, tm // ch, body, 0)


def _combine(pos_flat, h_tiles, gates, g2, b2, ys_tiles):
    n = h_tiles.shape[0] // SUBLANES
    d = SUBLANES * LANES
    tm = TM_ROWS
    row = lambda i, p: (i, 0)
    full2 = lambda i, p: (0, 0)
    return pl.pallas_call(
        functools.partial(_combine_kernel, tm=tm, nsteps=n // tm),
        grid_spec=pltpu.PrefetchScalarGridSpec(
            num_scalar_prefetch=1,
            grid=(n // tm,),
            in_specs=[pl.BlockSpec((tm * SUBLANES, LANES), row),
                      pl.BlockSpec((tm, TOP_K), row),
                      pl.BlockSpec(g2.shape, full2),
                      pl.BlockSpec(b2.shape, full2),
                      pl.BlockSpec(memory_space=pl.ANY)],
            out_specs=pl.BlockSpec((tm, d), row),
            scratch_shapes=[pltpu.VMEM((2, TOP_K, tm * SUBLANES, LANES), F32),
                            pltpu.SemaphoreType.DMA((2,))]),
        out_shape=jax.ShapeDtypeStruct((n, d), F32),
        compiler_params=pltpu.CompilerParams(
            dimension_semantics=("arbitrary",), vmem_limit_bytes=VMEM_LIMIT),
        name="combine",
    )(pos_flat, h_tiles, gates, g2, b2, ys_tiles)


def _layer(h2, seq, w_in, sinks, wab, wpg, ps, wpb, wo, g1, b1, wr, br, w1, bb1, w2, bb2, g2, b2):
    n, d = h2.shape
    q, k, v, p_in, g_attn, g_pool = _inproj(h2, w_in.astype(BF16))
    attn = _attention(q, k, v, sinks, seq)
    h1, idx, gates, rank, counts = _merge(
        h2, attn, p_in, g_attn, g_pool,
        wab.astype(BF16), wpg.astype(BF16), ps.reshape(1, -1), wpb.astype(BF16), wo.astype(BF16),
        g1.reshape(1, -1), b1.reshape(1, -1), wr.T, br.reshape(-1, 1), seq)
    idx, gates, rank = idx.T, gates.T, rank.T

    bm = BM_FFN
    counts = counts.reshape(N_EXPERTS)
    experts = jnp.arange(N_EXPERTS, dtype=jnp.int32)
    upto = experts[None, :] <= experts[:, None]

    def prefix_sum(v):
        return jnp.sum(jnp.where(upto, v[None, :], 0), axis=1)

    padded = ((counts + bm - 1) // bm) * bm
    pend = prefix_sum(padded)
    pstart = pend - padded
    n_rows = n * TOP_K + N_EXPERTS * bm
    nblk = n_rows // bm
    blk_lo = jnp.arange(nblk, dtype=jnp.int32) * bm
    blk_expert = jnp.minimum(
        jnp.sum((blk_lo[:, None] >= pend[None, :]).astype(jnp.int32), axis=1), N_EXPERTS - 1)
    blk_is = blk_expert[:, None] == experts[None, :]

    def per_block(table):
        return jnp.sum(jnp.where(blk_is, table[None, :], 0), axis=1).astype(jnp.int32)

    blk_valid = jnp.clip(per_block(pstart + counts) - blk_lo, 0, bm).astype(jnp.int32)
    per_blk = bm // ZERO_CHUNK
    chunk_end_in_blk = (jnp.arange(per_blk, dtype=jnp.int32) + 1) * ZERO_CHUNK
    chunk_has_padding = (chunk_end_in_blk[None, :] > blk_valid[:, None]).astype(jnp.int32).reshape(-1)
    n_used = (pend[-1:] // bm).astype(jnp.int32)
    nonempty = counts > 0
    group_slot = (prefix_sum(nonempty.astype(jnp.int32)) - 1) % 2
    later = jnp.where(nonempty[None, :] & ~upto, experts[None, :], N_EXPERTS)
    next_group = jnp.min(later, axis=1)
    next_group = jnp.where(next_group == N_EXPERTS, -1, next_group)
    blk_first = (blk_lo == per_block(pstart)).astype(jnp.int32)
    blk_slot = per_block(group_slot)
    blk_next = per_block(next_group)

    onehot = idx[:, :, None] == experts[None, None, :]
    pos = rank + jnp.sum(jnp.where(onehot, pstart[None, None, :], 0), axis=-1)
    pos_flat = pos.reshape(n * TOP_K).astype(jnp.int32)

    xs = _dispatch(pos_flat, chunk_has_padding, h1.reshape(n, SUBLANES, LANES), n_rows)
    ys = _ffn(blk_expert.astype(jnp.int32), blk_valid, blk_first, blk_slot, blk_next, n_used,
              xs.reshape(n_rows * SUBLANES, LANES),
              w1, bb1.reshape(N_EXPERTS, 1, -1), w2, bb2.reshape(N_EXPERTS, 1, -1))
    return _combine(pos_flat, h1, gates, g2.reshape(1, -1), b2.reshape(1, -1),
                    ys.reshape(n_rows, SUBLANES, LANES))


def kernel(x, w_in, attn_sinks, w_attn_branch, w_pool_group, pool_scale, w_pool_branch, w_out,
           ln1_g, ln1_b, w_router, b_router, w_mlp1, b_mlp1, w_mlp2, b_mlp2, ln2_g, ln2_b):
    bsz, seq, d = x.shape
    assert d == SUBLANES * LANES, "row-tile layout needs one (8, 128) f32 tile per token row"
    h = x.reshape(bsz * seq, d)
    for l in range(w_in.shape[0]):
        h = _layer(h, seq, w_in[l], attn_sinks[l], w_attn_branch[l], w_pool_group[l], pool_scale[l],
                   w_pool_branch[l], w_out[l], ln1_g[l], ln1_b[l], w_router[l], b_router[l],
                   w_mlp1[l], b_mlp1[l], w_mlp2[l], b_mlp2[l], ln2_g[l], ln2_b[l])
    return h.reshape(bsz, seq, d)
```

```python
import functools
import math

import jax
import jax.numpy as jnp
import numpy as np
from jax import lax
from jax.experimental import pallas as pl
from jax.experimental.pallas import tpu as pltpu

F32 = jnp.float32
BF16 = jnp.bfloat16

N_Q_HEADS = 8
N_KV_HEADS = 2
HEAD_DIM = 64
GROUP = N_Q_HEADS // N_KV_HEADS
ATTN_WIDTH = N_Q_HEADS * HEAD_DIM
KV_WIDTH = N_KV_HEADS * HEAD_DIM
WINDOW = 128
BLOCK = 128
POOL_WINDOWS = (2, 4, 8, 16)
POOL_GROUP_WIDTH = 128
POOL_WIDTH = len(POOL_WINDOWS) * POOL_GROUP_WIDTH
POOL_HALO = 8
N_EXPERTS = 32
TOP_K = 4
SWIGLU_LIMIT = 7.0
SWIGLU_ALPHA = 1.702
LN_EPS = 1e-5
DEPTH = 1
DEEPNORM_ALPHA = (2.0 * DEPTH) ** 0.25
NEG_BIG = -1e30

TM_PROJ = 512
ATTN_QB = 2
TM_MERGE = 512
TC_MERGE = 512
SUBLANES = 8
LANES = 128
TM_ROWS = 512
ROW_UNROLL = 2
COMBINE_CHUNK = 128
BM_FFN = 1024
FFN_ROW_STEP = 256
ZERO_CHUNK = 128
VMEM_LIMIT = 56 * 1024 * 1024


def _sigmoid(x):
    return 1.0 / (1.0 + jnp.exp(-x))


def _store_row_tiles(ref, value):
    rows = value.shape[0]
    for c in range(SUBLANES):
        ref[pl.ds(c, rows, stride=SUBLANES), :] = value[:, c * LANES:(c + 1) * LANES]


def _load_row_tiles(ref, rows):
    return jnp.concatenate(
        [ref[pl.ds(c, rows, stride=SUBLANES), :] for c in range(SUBLANES)], axis=-1)


def _layer_norm(x, g, b):
    mean = jnp.mean(x, axis=-1, keepdims=True)
    xc = x - mean
    var = jnp.mean(xc * xc, axis=-1, keepdims=True)
    return xc * lax.rsqrt(var + LN_EPS) * g + b


def _inproj_kernel(x_ref, w_ref, q_ref, k_ref, v_ref, p_ref, ga_ref, gp_ref, *, d_model):
    xb = x_ref[...].astype(BF16)
    o_k = ATTN_WIDTH
    o_v = o_k + KV_WIDTH
    o_p = o_v + KV_WIDTH
    o_g = o_p + POOL_WIDTH

    def proj(lo, hi):
        return jnp.dot(xb, w_ref[:, lo:hi], preferred_element_type=F32)

    q_ref[...] = (proj(0, o_k) * (1.0 / math.sqrt(HEAD_DIM))).astype(BF16)
    k_ref[...] = proj(o_k, o_v).astype(BF16)
    v_ref[...] = proj(o_v, o_p).astype(BF16)
    p_ref[...] = proj(o_p, o_g)
    ga_ref[...] = _sigmoid(proj(o_g, o_g + d_model))
    gp_ref[...] = _sigmoid(proj(o_g + d_model, o_g + 2 * d_model))


def _inproj(x2, w_in_b):
    n, d = x2.shape
    in_width = w_in_b.shape[1]
    tm = TM_PROJ
    row = lambda i: (i, 0)
    return pl.pallas_call(
        functools.partial(_inproj_kernel, d_model=d),
        grid=(n // tm,),
        in_specs=[pl.BlockSpec((tm, d), row),
                  pl.BlockSpec((d, in_width), lambda i: (0, 0))],
        out_specs=[pl.BlockSpec((tm, ATTN_WIDTH), row),
                   pl.BlockSpec((tm, KV_WIDTH), row),
                   pl.BlockSpec((tm, KV_WIDTH), row),
                   pl.BlockSpec((tm, POOL_WIDTH), row),
                   pl.BlockSpec((tm, d), row),
                   pl.BlockSpec((tm, d), row)],
        out_shape=[jax.ShapeDtypeStruct((n, ATTN_WIDTH), BF16),
                   jax.ShapeDtypeStruct((n, KV_WIDTH), BF16),
                   jax.ShapeDtypeStruct((n, KV_WIDTH), BF16),
                   jax.ShapeDtypeStruct((n, POOL_WIDTH), F32),
                   jax.ShapeDtypeStruct((n, d), F32),
                   jax.ShapeDtypeStruct((n, d), F32)],
        compiler_params=pltpu.CompilerParams(
            dimension_semantics=("arbitrary",), vmem_limit_bytes=VMEM_LIMIT),
        name="inproj",
    )(x2, w_in_b)


def _alibi_slope(h):
    return 2.0 ** (-8.0 * (h + 1) / N_Q_HEADS)


def _attn_kernel(sink_ref, q_ref, kp_ref, kc_ref, kn_ref, vp_ref, vc_ref, vn_ref, o_ref,
                 bias_ref, *, nb):
    j = pl.program_id(0) % nb
    cols = GROUP * BLOCK
    keys = 3 * BLOCK

    @pl.when(pl.program_id(0) == 0)
    def _():
        ki = lax.broadcasted_iota(jnp.int32, (keys, cols), 0)
        ci = lax.broadcasted_iota(jnp.int32, (keys, cols), 1)
        arel = jnp.abs(ki - BLOCK - ci % BLOCK)
        arel_f = arel.astype(F32)
        grp = ci // BLOCK
        inside = [(arel <= WINDOW) & (ki >= BLOCK), arel <= WINDOW, (arel <= WINDOW) & (ki < 2 * BLOCK)]
        for kvh in range(N_KV_HEADS):
            slope = jnp.full((keys, cols), _alibi_slope(kvh * GROUP + GROUP - 1), F32)
            for g in range(GROUP - 2, -1, -1):
                slope = jnp.where(grp == g, _alibi_slope(kvh * GROUP + g), slope)
            for edge in range(3):
                bias_ref[edge, kvh] = jnp.where(inside[edge], -(slope * arel_f), NEG_BIG)

    kk = jnp.concatenate([kp_ref[...], kc_ref[...], kn_ref[...]], axis=0)
    vv = jnp.concatenate([vp_ref[...], vc_ref[...], vn_ref[...]], axis=0)
    grp_row = lax.broadcasted_iota(jnp.int32, (1, cols), 1) // BLOCK

    for qb in range(ATTN_QB):
        edge = 1
        if qb == ATTN_QB - 1:
            edge = jnp.where(j == nb - 1, 2, edge)
        if qb == 0:
            edge = jnp.where(j == 0, 0, edge)
        qrows = slice(qb * BLOCK, (qb + 1) * BLOCK)
        krows = slice(qb * BLOCK, qb * BLOCK + keys)
        for kvh in range(N_KV_HEADS):
            heads = [kvh * GROUP + g for g in range(GROUP)]
            sink = jnp.full((1, cols), sink_ref[heads[GROUP - 1]], F32)
            for g in range(GROUP - 2, -1, -1):
                sink = jnp.where(grp_row == g, sink_ref[heads[g]], sink)
            qs = jnp.concatenate(
                [q_ref[qrows, h * HEAD_DIM:(h + 1) * HEAD_DIM] for h in heads], axis=0)
            kh = kk[krows, kvh * HEAD_DIM:(kvh + 1) * HEAD_DIM]
            vh = vv[krows, kvh * HEAD_DIM:(kvh + 1) * HEAD_DIM]
            s = lax.dot_general(kh, qs, (((1,), (1,)), ((), ())), preferred_element_type=F32)
            s = s + bias_ref[edge, kvh]
            m = jnp.maximum(jnp.max(s, axis=0, keepdims=True), sink)
            p = jnp.exp(s - m)
            denom = jnp.sum(p, axis=0, keepdims=True) + jnp.exp(sink - m)
            ot = lax.dot_general(vh, p.astype(BF16), (((0,), (0,)), ((), ())),
                                 preferred_element_type=F32) / denom
            o = ot.T.astype(BF16)
            for g, h in enumerate(heads):
                o_ref[qrows, h * HEAD_DIM:(h + 1) * HEAD_DIM] = o[g * BLOCK:(g + 1) * BLOCK]


def _attention(q, k, v, sinks, seq):
    n = q.shape[0]
    tq = ATTN_QB * BLOCK
    nb = seq // tq
    assert nb > 1 and seq % tq == 0

    def prev_map(i, s):
        return (jnp.where(i % nb == 0, i * ATTN_QB, i * ATTN_QB - 1), 0)

    def next_map(i, s):
        return (jnp.where(i % nb == nb - 1, i * ATTN_QB, (i + 1) * ATTN_QB), 0)

    cur_map = lambda i, s: (i, 0)
    kv_spec = lambda m: pl.BlockSpec((BLOCK, KV_WIDTH), m)
    kv_cur = pl.BlockSpec((tq, KV_WIDTH), cur_map)
    return pl.pallas_call(
        functools.partial(_attn_kernel, nb=nb),
        grid_spec=pltpu.PrefetchScalarGridSpec(
            num_scalar_prefetch=1,
            grid=(n // tq,),
            in_specs=[pl.BlockSpec((tq, ATTN_WIDTH), cur_map),
                      kv_spec(prev_map), kv_cur, kv_spec(next_map),
                      kv_spec(prev_map), kv_cur, kv_spec(next_map)],
            out_specs=pl.BlockSpec((tq, ATTN_WIDTH), cur_map),
            scratch_shapes=[pltpu.VMEM((3, N_KV_HEADS, 3 * BLOCK, GROUP * BLOCK), F32)]),
        out_shape=jax.ShapeDtypeStruct((n, ATTN_WIDTH), BF16),
        compiler_params=pltpu.CompilerParams(
            dimension_semantics=("arbitrary",), vmem_limit_bytes=VMEM_LIMIT),
        name="attn",
    )(sinks, q, k, k, k, v, v, v)


def _merge_kernel(x_ref, a_ref, pc_ref, pp_ref, pn_ref, ga_ref, gp_ref,
                  wab_ref, wpg_ref, ps_ref, wpb_ref, wo_ref, g1_ref, b1_ref, wr_ref, br_ref,
                  h_ref, idx_ref, gate_ref, rank_ref, cnt_ref,
                  ext_ref, carry_ref, earlier_ref, *, seq, tm, tc):
    i = pl.program_id(0)
    tiles_per_seq = seq // tm
    jt = i % tiles_per_seq

    @pl.when(i == 0)
    def _():
        carry_ref[...] = jnp.zeros_like(carry_ref)
        r_i = lax.broadcasted_iota(jnp.int32, (tc, tc), 0)
        c_i = lax.broadcasted_iota(jnp.int32, (tc, tc), 1)
        earlier_ref[...] = (r_i < c_i).astype(BF16)

    ext_ref[0:POOL_HALO, :] = jnp.where(jt > 0, pp_ref[...], 0.0)
    ext_ref[POOL_HALO:POOL_HALO + tm, :] = pc_ref[...]
    ext_ref[POOL_HALO + tm:, :] = jnp.where(jt < tiles_per_seq - 1, pn_ref[...], 0.0)

    w_hi = wr_ref[...].astype(BF16)
    w_lo = (wr_ref[...] - w_hi.astype(F32)).astype(BF16)
    w_split = jnp.concatenate([w_hi, w_lo], axis=0)
    carry = carry_ref[...]

    for r0 in range(0, tm, tc):
        rs = slice(r0, r0 + tc)
        cl = slice(r0, r0 + tc)

        spos = jt * tm + r0 + lax.broadcasted_iota(jnp.int32, (tc, 1), 0)
        pooled_parts = []
        for g, w in enumerate(POOL_WINDOWS):
            cs = slice(g * POOL_GROUP_WIDTH, (g + 1) * POOL_GROUP_WIDTH)
            acc = None
            for off in range(-(w // 2), w // 2):
                lo_row = POOL_HALO + r0 + off
                t = ext_ref[lo_row:lo_row + tc, cs]
                acc = t if acc is None else acc + t
            hi = jnp.minimum(spos + (w // 2 - 1), seq - 1)
            lo = jnp.maximum(spos - w // 2, 0)
            cnt = (hi - lo + 1).astype(F32)
            y = acc / cnt - pc_ref[rs, cs]
            yg = jnp.dot(y.astype(BF16), wpg_ref[g], preferred_element_type=F32)
            pooled_parts.append(yg * ps_ref[:, cs])
        pool_feat = jnp.concatenate(pooled_parts, axis=-1).astype(BF16)

        a_br = jnp.dot(a_ref[rs, :], wab_ref[...], preferred_element_type=F32)
        p_br = jnp.dot(pool_feat, wpb_ref[...], preferred_element_type=F32)
        mixed = ga_ref[rs, :] * a_br + gp_ref[rs, :] * p_br
        o = jnp.dot(mixed.astype(BF16), wo_ref[...], preferred_element_type=F32)
        h = _layer_norm(DEEPNORM_ALPHA * x_ref[rs, :] + o, g1_ref[...], b1_ref[...])
        for c in range(SUBLANES):
            h_ref[pl.ds(r0 * SUBLANES + c, tc, stride=SUBLANES), :] = h[:, c * LANES:(c + 1) * LANES]

        h_hi = h.astype(BF16)
        h_lo = (h - h_hi.astype(F32)).astype(BF16)
        nt = (((1,), (1,)), ((), ()))
        part = lax.dot_general(w_split, h_hi, nt, preferred_element_type=F32)
        logits = (part[:N_EXPERTS] + part[N_EXPERTS:]
                  + lax.dot_general(w_hi, h_lo, nt, preferred_element_type=F32)
                  + br_ref[...])

        eidx = lax.broadcasted_iota(jnp.int32, (N_EXPERTS, tc), 0)
        work = logits
        sel_idx, sel_val, sel_hit = [], [], []
        chosen = jnp.zeros((N_EXPERTS, tc), F32)
        for _ in range(TOP_K):
            m = jnp.max(work, axis=0, keepdims=True)
            ix = jnp.min(jnp.where(work == m, eidx, N_EXPERTS), axis=0, keepdims=True)
            hit = eidx == ix
            sel_idx.append(ix)
            sel_val.append(m)
            sel_hit.append(hit)
            chosen = jnp.where(hit, 1.0, chosen)
            work = jnp.where(hit, -jnp.inf, work)
        ex = [jnp.exp(v - sel_val[0]) for v in sel_val]
        tot = ex[0] + ex[1] + ex[2] + ex[3]
        gate_ref[:, cl] = jnp.concatenate([e / tot for e in ex], axis=0)
        idx_ref[:, cl] = jnp.concatenate(sel_idx, axis=0)

        before = jnp.dot(chosen.astype(BF16), earlier_ref[...], preferred_element_type=F32) + carry
        ranks = [jnp.sum(jnp.where(hit, before, 0.0), axis=0, keepdims=True) for hit in sel_hit]
        rank_ref[:, cl] = jnp.concatenate(ranks, axis=0).astype(jnp.int32)
        carry = carry + jnp.sum(chosen, axis=1, keepdims=True)

    carry_ref[...] = carry
    cnt_ref[...] = carry.astype(jnp.int32)


def _merge(x2, attn, p_in, g_attn, g_pool, wab, wpg, ps, wpb, wo, g1, b1, wr, br, seq):
    n, d = x2.shape
    tm = TM_MERGE
    halo_blocks = tm // POOL_HALO
    row = lambda i: (i, 0)
    col = lambda i: (0, i)
    full2 = lambda i: (0, 0)

    def prev_halo(i):
        return (jnp.maximum(i * halo_blocks - 1, 0), 0)

    def next_halo(i):
        return (jnp.minimum((i + 1) * halo_blocks, n // POOL_HALO - 1), 0)

    return pl.pallas_call(
        functools.partial(_merge_kernel, seq=seq, tm=tm, tc=TC_MERGE),
        grid=(n // tm,),
        in_specs=[pl.BlockSpec((tm, d), row),
                  pl.BlockSpec((tm, ATTN_WIDTH), row),
                  pl.BlockSpec((tm, POOL_WIDTH), row),
                  pl.BlockSpec((POOL_HALO, POOL_WIDTH), prev_halo),
                  pl.BlockSpec((POOL_HALO, POOL_WIDTH), next_halo),
                  pl.BlockSpec((tm, d), row),
                  pl.BlockSpec((tm, d), row),
                  pl.BlockSpec(wab.shape, full2),
                  pl.BlockSpec(wpg.shape, lambda i: (0, 0, 0)),
                  pl.BlockSpec(ps.shape, full2),
                  pl.BlockSpec(wpb.shape, full2),
                  pl.BlockSpec(wo.shape, full2),
                  pl.BlockSpec(g1.shape, full2),
                  pl.BlockSpec(b1.shape, full2),
                  pl.BlockSpec(wr.shape, full2),
                  pl.BlockSpec(br.shape, full2)],
        out_specs=[pl.BlockSpec((tm * SUBLANES, LANES), row),
                   pl.BlockSpec((TOP_K, tm), col),
                   pl.BlockSpec((TOP_K, tm), col),
                   pl.BlockSpec((TOP_K, tm), col),
                   pl.BlockSpec((N_EXPERTS, 1), full2)],
        out_shape=[jax.ShapeDtypeStruct((n * SUBLANES, LANES), F32),
                   jax.ShapeDtypeStruct((TOP_K, n), jnp.int32),
                   jax.ShapeDtypeStruct((TOP_K, n), F32),
                   jax.ShapeDtypeStruct((TOP_K, n), jnp.int32),
                   jax.ShapeDtypeStruct((N_EXPERTS, 1), jnp.int32)],
        scratch_shapes=[pltpu.VMEM((tm + 2 * POOL_HALO, POOL_WIDTH), F32),
                        pltpu.VMEM((N_EXPERTS, 1), F32),
                        pltpu.VMEM((TC_MERGE, TC_MERGE), BF16)],
        compiler_params=pltpu.CompilerParams(
            dimension_semantics=("arbitrary",), vmem_limit_bytes=VMEM_LIMIT),
        name="merge",
    )(x2, attn, p_in, p_in, p_in, g_attn, g_pool, wab, wpg, ps, wpb, wo, g1, b1, wr, br)


def _dispatch_kernel(pos_ref, zfill_ref, h_ref, xs_ref, zero_ref, tile_ref, sem, fsem, zsem,
                     *, tm, nchunks, nsteps):
    i = pl.program_id(0)
    base = i * (tm * TOP_K)

    @pl.when(pl.program_id(0) == 0)
    def _():
        zero_ref[...] = jnp.zeros_like(zero_ref)

        def flagged_chunks(op):
            def body(c, carry):
                @pl.when(zfill_ref[c] == 1)
                def _():
                    start = pl.multiple_of(c * ZERO_CHUNK, ZERO_CHUNK)
                    op(pltpu.make_async_copy(zero_ref, xs_ref.at[pl.ds(start, ZERO_CHUNK)], zsem))
                return carry
            lax.fori_loop(0, nchunks, body, 0)

        flagged_chunks(lambda c: c.start())
        flagged_chunks(lambda c: c.wait())

    slot = i % 3
    nxt = (i + 1) % 3

    def fetch(step, s):
        src = h_ref.at[pl.ds(pl.multiple_of(step * tm, tm), tm)]
        return pltpu.make_async_copy(src, tile_ref.at[s], fsem.at[s])

    def wait_rows(s):
        for _ in range(TOP_K):
            pltpu.make_async_copy(tile_ref.at[s], xs_ref.at[pl.ds(0, tm)], sem.at[s]).wait()

    @pl.when(i == 0)
    def _():
        fetch(0, 0).start()

    @pl.when(i + 1 < nsteps)
    def _():
        @pl.when(i >= 2)
        def _():
            wait_rows(nxt)
        fetch(i + 1, nxt).start()

    fetch(i, slot).wait()

    def issue(tt, c):
        for u in range(ROW_UNROLL):
            for k in range(TOP_K):
                t = tt * ROW_UNROLL + u
                dst = pos_ref[base + t * TOP_K + k]
                pltpu.make_async_copy(tile_ref.at[slot, t], xs_ref.at[dst],
                                      sem.at[slot]).start(priority=k % 2)
        return c

    lax.fori_loop(0, tm // ROW_UNROLL, issue, 0)

    @pl.when(i == nsteps - 1)
    def _():
        for s in range(3):
            wait_rows(s)


def _dispatch(pos_flat, chunk_has_padding, h_tiles, n_rows):
    n = h_tiles.shape[0]
    tm = TM_ROWS
    assert n // tm >= 3, "the staging ring assumes at least three token tiles"
    return pl.pallas_call(
        functools.partial(_dispatch_kernel, tm=tm, nchunks=n_rows // ZERO_CHUNK, nsteps=n // tm),
        grid_spec=pltpu.PrefetchScalarGridSpec(
            num_scalar_prefetch=2,
            grid=(n // tm,),
            in_specs=[pl.BlockSpec(memory_space=pl.ANY)],
            out_specs=pl.BlockSpec(memory_space=pl.ANY),
            scratch_shapes=[pltpu.VMEM((ZERO_CHUNK, SUBLANES, LANES), F32),
                            pltpu.VMEM((3, tm, SUBLANES, LANES), F32),
                            pltpu.SemaphoreType.DMA((3,)),
                            pltpu.SemaphoreType.DMA((3,)),
                            pltpu.SemaphoreType.DMA]),
        out_shape=jax.ShapeDtypeStruct((n_rows, SUBLANES, LANES), F32),
        compiler_params=pltpu.CompilerParams(
            dimension_semantics=("arbitrary",), vmem_limit_bytes=VMEM_LIMIT),
        name="dispatch",
    )(pos_flat, chunk_has_padding, h_tiles)


def _ffn_kernel(be_ref, nv_ref, first_ref, nxt_ref, nu_ref,
                x_ref, w1_hbm, b1_ref, w2_hbm, b2_ref, y_ref,
                w1f_ref, w2f_ref, w1b_ref, w2b_ref, wsem, *, bm, f):
    b = pl.program_id(0)
    used = b < nu_ref[0]

    def fetch(e):
        return (pltpu.make_async_copy(w1_hbm.at[e], w1f_ref, wsem.at[0]),
                pltpu.make_async_copy(w2_hbm.at[e], w2f_ref, wsem.at[1]))

    @pl.when(b == 0)
    def _():
        for c in fetch(be_ref[0]):
            c.start()

    @pl.when(used & (first_ref[b] == 1))
    def _():
        for c in fetch(be_ref[b]):
            c.wait()
        w1b_ref[...] = w1f_ref[...].astype(BF16)
        w2b_ref[...] = w2f_ref[...].astype(BF16)

        @pl.when(nxt_ref[b] >= 0)
        def _():
            for c in fetch(nxt_ref[b]):
                c.start()

    def mlp(rows):
        ridx = lax.broadcasted_iota(jnp.int32, (rows, 1), 0)
        x = jnp.where(ridx < nv_ref[b], _load_row_tiles(x_ref, rows), 0.0).astype(BF16)
        h = jnp.dot(x, w1b_ref[...], preferred_element_type=F32) + b1_ref[0]
        gate = jnp.minimum(h[:, :f], SWIGLU_LIMIT)
        up = jnp.clip(h[:, f:], -SWIGLU_LIMIT, SWIGLU_LIMIT)
        act = (up + 1.0) * gate * _sigmoid(SWIGLU_ALPHA * gate)
        y = jnp.dot(act.astype(BF16), w2b_ref[...], preferred_element_type=F32) + b2_ref[0]
        _store_row_tiles(y_ref, y)

    for rows in range(FFN_ROW_STEP, bm + 1, FFN_ROW_STEP):
        @pl.when(used & (nv_ref[b] > rows - FFN_ROW_STEP) & (nv_ref[b] <= rows))
        def _(rows=rows):
            mlp(rows)
            if rows < bm:
                y_ref[rows * SUBLANES:, :] = jnp.zeros(((bm - rows) * SUBLANES, LANES), F32)

    @pl.when(b >= nu_ref[0])
    def _():
        y_ref[...] = jnp.zeros_like(y_ref)


def _ffn(blk_expert, blk_valid, blk_first, blk_next, n_used, xs_tiles, w1, b1, w2, b2):
    n_rows = xs_tiles.shape[0] // SUBLANES
    bm = BM_FFN
    d, f = w1.shape[1], w2.shape[1]

    def row_map(b, be, nv, fi, nx, nu):
        return (jnp.minimum(b, nu[0] - 1), 0)

    def exp_map(b, be, nv, fi, nx, nu):
        return (be[b], 0, 0)

    return pl.pallas_call(
        functools.partial(_ffn_kernel, bm=bm, f=f),
        grid_spec=pltpu.PrefetchScalarGridSpec(
            num_scalar_prefetch=5,
            grid=(n_rows // bm,),
            in_specs=[pl.BlockSpec((bm * SUBLANES, LANES), row_map),
                      pl.BlockSpec(memory_space=pl.ANY),
                      pl.BlockSpec((1, 1, 2 * f), exp_map),
                      pl.BlockSpec(memory_space=pl.ANY),
                      pl.BlockSpec((1, 1, d), exp_map)],
            out_specs=pl.BlockSpec((bm * SUBLANES, LANES), lambda b, *_: (b, 0)),
            scratch_shapes=[pltpu.VMEM((d, 2 * f), F32),
                            pltpu.VMEM((f, d), F32),
                            pltpu.VMEM((d, 2 * f), BF16),
                            pltpu.VMEM((f, d), BF16),
                            pltpu.SemaphoreType.DMA((2,))]),
        out_shape=jax.ShapeDtypeStruct((n_rows * SUBLANES, LANES), F32),
        compiler_params=pltpu.CompilerParams(
            dimension_semantics=("arbitrary",), vmem_limit_bytes=VMEM_LIMIT),
        name="ffn",
    )(blk_expert, blk_valid, blk_first, blk_next, n_used, xs_tiles, w1, b1, w2, b2)


def _combine_kernel(pos_ref, h_ref, gate_ref, g2_ref, b2_ref, ys_ref, o_ref, buf_ref, sem, *, tm, nsteps):
    i = pl.program_id(0)
    slot = i % 2
    ch = COMBINE_CHUNK

    def request_rows(step, s, t0, count):
        base = step * (tm * TOP_K)
        for u in range(count):
            for k in range(TOP_K):
                src = pos_ref[base + (t0 + u) * TOP_K + k]
                row0 = pl.multiple_of((t0 + u) * SUBLANES, SUBLANES)
                dst = buf_ref.at[s, k, pl.ds(row0, SUBLANES), :]
                pltpu.make_async_copy(ys_ref.at[src], dst, sem.at[s]).start(priority=k % 2)

    def compute_rows(r0):
        def rows_of(ref):
            return jnp.concatenate(
                [ref[pl.ds(r0 * SUBLANES + c, ch, stride=SUBLANES), :] for c in range(SUBLANES)], axis=-1)

        gates = gate_ref[pl.ds(r0, ch), :]
        f = gates[:, 0:1] * rows_of(buf_ref.at[slot, 0])
        for k in range(1, TOP_K):
            f = f + gates[:, k:k + 1] * rows_of(buf_ref.at[slot, k])
        o_ref[pl.ds(r0, ch), :] = _layer_norm(DEEPNORM_ALPHA * rows_of(h_ref) + f, g2_ref[...], b2_ref[...])

    @pl.when(i == 0)
    def _():
        def issue(c, carry):
            request_rows(0, 0, c * ROW_UNROLL, ROW_UNROLL)
            return carry
        lax.fori_loop(0, tm // ROW_UNROLL, issue, 0)

    for k in range(TOP_K):
        pltpu.make_async_copy(h_ref, buf_ref.at[slot, k], sem.at[slot]).wait()

    @pl.when(i + 1 < nsteps)
    def _():
        def body(c, carry):
            r0 = pl.multiple_of(c * ch, ch)
            request_rows(i + 1, 1 - slot, r0, ch)
            compute_rows(r0)
            return carry
        lax.fori_loop(0, tm // ch, body, 0)

    @pl.when(i + 1 >= nsteps)
    def _():
        def body(c, carry):
            compute_rows(pl.multiple_of(c * ch, ch))
            return carry
        lax.fori_loop(0, tm // ch, body, 0)


def _combine(pos_flat, h_tiles, gates, g2, b2, ys_tiles):
    n = h_tiles.shape[0] // SUBLANES
    d = SUBLANES * LANES
    tm = TM_ROWS
    row = lambda i, p: (i, 0)
    full2 = lambda i, p: (0, 0)
    return pl.pallas_call(
        functools.partial(_combine_kernel, tm=tm, nsteps=n // tm),
        grid_spec=pltpu.PrefetchScalarGridSpec(
            num_scalar_prefetch=1,
            grid=(n // tm,),
            in_specs=[pl.BlockSpec((tm * SUBLANES, LANES), row),
                      pl.BlockSpec((tm, TOP_K), row),
                      pl.BlockSpec(g2.shape, full2),
                      pl.BlockSpec(b2.shape, full2),
                      pl.BlockSpec(memory_space=pl.ANY)],
            out_specs=pl.BlockSpec((tm, d), row),
            scratch_shapes=[pltpu.VMEM((2, TOP_K, tm * SUBLANES, LANES), F32),
                            pltpu.SemaphoreType.DMA((2,))]),
        out_shape=jax.ShapeDtypeStruct((n, d), F32),
        compiler_params=pltpu.CompilerParams(
            dimension_semantics=("arbitrary",), vmem_limit_bytes=VMEM_LIMIT),
        name="combine",
    )(pos_flat, h_tiles, gates, g2, b2, ys_tiles)


def _layer(h2, seq, w_in, sinks, wab, wpg, ps, wpb, wo, g1, b1, wr, br, w1, bb1, w2, bb2, g2, b2):
    n, d = h2.shape
    q, k, v, p_in, g_attn, g_pool = _inproj(h2, w_in.astype(BF16))
    attn = _attention(q, k, v, sinks, seq)
    h1, idx, gates, rank, counts = _merge(
        h2, attn, p_in, g_attn, g_pool,
        wab.astype(BF16), wpg.astype(BF16), ps.reshape(1, -1), wpb.astype(BF16), wo.astype(BF16),
        g1.reshape(1, -1), b1.reshape(1, -1), wr.T, br.reshape(-1, 1), seq)
    idx, gates, rank = idx.T, gates.T, rank.T

    bm = BM_FFN
    counts = counts.reshape(N_EXPERTS)
    experts = jnp.arange(N_EXPERTS, dtype=jnp.int32)
    upto = experts[None, :] <= experts[:, None]

    def prefix_sum(v):
        return jnp.sum(jnp.where(upto, v[None, :], 0), axis=1)

    padded = ((counts + bm - 1) // bm) * bm
    pend = prefix_sum(padded)
    pstart = pend - padded
    n_rows = n * TOP_K + N_EXPERTS * bm
    nblk = n_rows // bm
    blk_lo = jnp.arange(nblk, dtype=jnp.int32) * bm
    blk_expert = jnp.minimum(
        jnp.sum((blk_lo[:, None] >= pend[None, :]).astype(jnp.int32), axis=1), N_EXPERTS - 1)
    blk_is = blk_expert[:, None] == experts[None, :]

    def per_block(table):
        return jnp.sum(jnp.where(blk_is, table[None, :], 0), axis=1).astype(jnp.int32)

    blk_valid = jnp.clip(per_block(pstart + counts) - blk_lo, 0, bm).astype(jnp.int32)
    per_blk = bm // ZERO_CHUNK
    chunk_end_in_blk = (jnp.arange(per_blk, dtype=jnp.int32) + 1) * ZERO_CHUNK
    chunk_has_padding = (chunk_end_in_blk[None, :] > blk_valid[:, None]).astype(jnp.int32).reshape(-1)
    n_used = (pend[-1:] // bm).astype(jnp.int32)
    nonempty = counts > 0
    later = jnp.where(nonempty[None, :] & ~upto, experts[None, :], N_EXPERTS)
    next_group = jnp.min(later, axis=1)
    next_group = jnp.where(next_group == N_EXPERTS, -1, next_group)
    blk_first = (blk_lo == per_block(pstart)).astype(jnp.int32)
    blk_next = per_block(next_group)

    onehot = idx[:, :, None] == experts[None, None, :]
    pos = rank + jnp.sum(jnp.where(onehot, pstart[None, None, :], 0), axis=-1)
    pos_flat = pos.reshape(n * TOP_K).astype(jnp.int32)

    xs = _dispatch(pos_flat, chunk_has_padding, h1.reshape(n, SUBLANES, LANES), n_rows)
    ys = _ffn(blk_expert.astype(jnp.int32), blk_valid, blk_first, blk_next, n_used,
              xs.reshape(n_rows * SUBLANES, LANES),
              w1, bb1.reshape(N_EXPERTS, 1, -1), w2, bb2.reshape(N_EXPERTS, 1, -1))
    return _combine(pos_flat, h1, gates, g2.reshape(1, -1), b2.reshape(1, -1),
                    ys.reshape(n_rows, SUBLANES, LANES))


def kernel(x, w_in, attn_sinks, w_attn_branch, w_pool_group, pool_scale, w_pool_branch, w_out,
           ln1_g, ln1_b, w_router, b_router, w_mlp1, b_mlp1, w_mlp2, b_mlp2, ln2_g, ln2_b):
    bsz, seq, d = x.shape
    assert d == SUBLANES * LANES, "row-tile layout needs one (8, 128) f32 tile per token row"
    h = x.reshape(bsz * seq, d)
    for l in range(w_in.shape[0]):
        h = _layer(h, seq, w_in[l], attn_sinks[l], w_attn_branch[l], w_pool_group[l], pool_scale[l],
                   w_pool_branch[l], w_out[l], ln1_g[l], ln1_b[l], w_router[l], b_router[l],
                   w_mlp1[l], b_mlp1[l], w_mlp2[l], b_mlp2[l], ln2_g[l], ln2_b[l])
    return h.reshape(bsz, seq, d)
```

```python
import functools
import math

import jax
import jax.numpy as jnp
import numpy as np
from jax import lax
from jax.experimental import pallas as pl
from jax.experimental.pallas import tpu as pltpu

F32 = jnp.float32
BF16 = jnp.bfloat16

N_Q_HEADS = 8
N_KV_HEADS = 2
HEAD_DIM = 64
GROUP = N_Q_HEADS // N_KV_HEADS
ATTN_WIDTH = N_Q_HEADS * HEAD_DIM
KV_WIDTH = N_KV_HEADS * HEAD_DIM
WINDOW = 128
BLOCK = 128
POOL_WINDOWS = (2, 4, 8, 16)
POOL_GROUP_WIDTH = 128
POOL_WIDTH = len(POOL_WINDOWS) * POOL_GROUP_WIDTH
POOL_HALO = 8
N_EXPERTS = 32
TOP_K = 4
SWIGLU_LIMIT = 7.0
SWIGLU_ALPHA = 1.702
LN_EPS = 1e-5
DEPTH = 1
DEEPNORM_ALPHA = (2.0 * DEPTH) ** 0.25
NEG_BIG = -1e30

TM_PROJ = 1024
ATTN_QB = 8
TM_MERGE = 512
TC_MERGE = 512
SUBLANES = 8
LANES = 128
TM_ROWS = 512
ROW_UNROLL = 2
COMBINE_CHUNK = 128
BM_FFN = 512
FFN_ROW_STEP = 256
ZERO_CHUNK = 128
VMEM_LIMIT = 56 * 1024 * 1024


def _sigmoid(x):
    return 1.0 / (1.0 + jnp.exp(-x))


def _store_row_tiles(ref, value):
    rows = value.shape[0]
    for c in range(SUBLANES):
        ref[pl.ds(c, rows, stride=SUBLANES), :] = value[:, c * LANES:(c + 1) * LANES]


def _load_row_tiles(ref, rows):
    return jnp.concatenate(
        [ref[pl.ds(c, rows, stride=SUBLANES), :] for c in range(SUBLANES)], axis=-1)


def _layer_norm(x, g, b):
    mean = jnp.mean(x, axis=-1, keepdims=True)
    xc = x - mean
    var = jnp.mean(xc * xc, axis=-1, keepdims=True)
    return xc * lax.rsqrt(var + LN_EPS) * g + b


def _inproj_kernel(x_ref, w_ref, q_ref, k_ref, v_ref, p_ref, ga_ref, gp_ref, *, d_model):
    xb = x_ref[...].astype(BF16)
    o_k = ATTN_WIDTH
    o_v = o_k + KV_WIDTH
    o_p = o_v + KV_WIDTH
    o_g = o_p + POOL_WIDTH

    def proj(lo, hi):
        return jnp.dot(xb, w_ref[:, lo:hi], preferred_element_type=F32)

    q_ref[...] = (proj(0, o_k) * (1.0 / math.sqrt(HEAD_DIM))).astype(BF16)
    k_ref[...] = proj(o_k, o_v).astype(BF16)
    v_ref[...] = proj(o_v, o_p).astype(BF16)
    p_ref[...] = proj(o_p, o_g)
    ga_ref[...] = _sigmoid(proj(o_g, o_g + d_model))
    gp_ref[...] = _sigmoid(proj(o_g + d_model, o_g + 2 * d_model))


def _inproj(x2, w_in_b):
    n, d = x2.shape
    in_width = w_in_b.shape[1]
    tm = TM_PROJ
    row = lambda i: (i, 0)
    return pl.pallas_call(
        functools.partial(_inproj_kernel, d_model=d),
        grid=(n // tm,),
        in_specs=[pl.BlockSpec((tm, d), row),
                  pl.BlockSpec((d, in_width), lambda i: (0, 0))],
        out_specs=[pl.BlockSpec((tm, ATTN_WIDTH), row),
                   pl.BlockSpec((tm, KV_WIDTH), row),
                   pl.BlockSpec((tm, KV_WIDTH), row),
                   pl.BlockSpec((tm, POOL_WIDTH), row),
                   pl.BlockSpec((tm, d), row),
                   pl.BlockSpec((tm, d), row)],
        out_shape=[jax.ShapeDtypeStruct((n, ATTN_WIDTH), BF16),
                   jax.ShapeDtypeStruct((n, KV_WIDTH), BF16),
                   jax.ShapeDtypeStruct((n, KV_WIDTH), BF16),
                   jax.ShapeDtypeStruct((n, POOL_WIDTH), F32),
                   jax.ShapeDtypeStruct((n, d), F32),
                   jax.ShapeDtypeStruct((n, d), F32)],
        compiler_params=pltpu.CompilerParams(
            dimension_semantics=("arbitrary",), vmem_limit_bytes=VMEM_LIMIT),
        name="inproj",
    )(x2, w_in_b)


def _alibi_slope(h):
    return 2.0 ** (-8.0 * (h + 1) / N_Q_HEADS)


def _attn_kernel(sink_ref, q_ref, kp_ref, kc_ref, kn_ref, vp_ref, vc_ref, vn_ref, o_ref,
                 bias_ref, *, nb):
    j = pl.program_id(0) % nb
    cols = GROUP * BLOCK
    keys = 3 * BLOCK

    @pl.when(pl.program_id(0) == 0)
    def _():
        ki = lax.broadcasted_iota(jnp.int32, (keys, cols), 0)
        ci = lax.broadcasted_iota(jnp.int32, (keys, cols), 1)
        arel = jnp.abs(ki - BLOCK - ci % BLOCK)
        arel_f = arel.astype(F32)
        grp = ci // BLOCK
        inside = [(arel <= WINDOW) & (ki >= BLOCK), arel <= WINDOW, (arel <= WINDOW) & (ki < 2 * BLOCK)]
        for kvh in range(N_KV_HEADS):
            slope = jnp.full((keys, cols), _alibi_slope(kvh * GROUP + GROUP - 1), F32)
            for g in range(GROUP - 2, -1, -1):
                slope = jnp.where(grp == g, _alibi_slope(kvh * GROUP + g), slope)
            for edge in range(3):
                bias_ref[edge, kvh] = jnp.where(inside[edge], -(slope * arel_f), NEG_BIG)

    kk = jnp.concatenate([kp_ref[...], kc_ref[...], kn_ref[...]], axis=0)
    vv = jnp.concatenate([vp_ref[...], vc_ref[...], vn_ref[...]], axis=0)
    grp_row = lax.broadcasted_iota(jnp.int32, (1, cols), 1) // BLOCK

    for qb in range(ATTN_QB):
        edge = 1
        if qb == ATTN_QB - 1:
            edge = jnp.where(j == nb - 1, 2, edge)
        if qb == 0:
            edge = jnp.where(j == 0, 0, edge)
        qrows = slice(qb * BLOCK, (qb + 1) * BLOCK)
        krows = slice(qb * BLOCK, qb * BLOCK + keys)
        for kvh in range(N_KV_HEADS):
            heads = [kvh * GROUP + g for g in range(GROUP)]
            sink = jnp.full((1, cols), sink_ref[heads[GROUP - 1]], F32)
            for g in range(GROUP - 2, -1, -1):
                sink = jnp.where(grp_row == g, sink_ref[heads[g]], sink)
            qs = jnp.concatenate(
                [q_ref[qrows, h * HEAD_DIM:(h + 1) * HEAD_DIM] for h in heads], axis=0)
            kh = kk[krows, kvh * HEAD_DIM:(kvh + 1) * HEAD_DIM]
            vh = vv[krows, kvh * HEAD_DIM:(kvh + 1) * HEAD_DIM]
            s = lax.dot_general(kh, qs, (((1,), (1,)), ((), ())), preferred_element_type=F32)
            s = s + bias_ref[edge, kvh]
            m = jnp.maximum(jnp.max(s, axis=0, keepdims=True), sink)
            p = jnp.exp(s - m)
            denom = jnp.sum(p, axis=0, keepdims=True) + jnp.exp(sink - m)
            ot = lax.dot_general(vh, p.astype(BF16), (((0,), (0,)), ((), ())),
                                 preferred_element_type=F32) / denom
            o = ot.T.astype(BF16)
            for g, h in enumerate(heads):
                o_ref[qrows, h * HEAD_DIM:(h + 1) * HEAD_DIM] = o[g * BLOCK:(g + 1) * BLOCK]


def _attention(q, k, v, sinks, seq):
    n = q.shape[0]
    tq = ATTN_QB * BLOCK
    nb = seq // tq
    assert nb > 1 and seq % tq == 0

    def prev_map(i, s):
        return (jnp.where(i % nb == 0, i * ATTN_QB, i * ATTN_QB - 1), 0)

    def next_map(i, s):
        return (jnp.where(i % nb == nb - 1, i * ATTN_QB, (i + 1) * ATTN_QB), 0)

    cur_map = lambda i, s: (i, 0)
    kv_spec = lambda m: pl.BlockSpec((BLOCK, KV_WIDTH), m)
    kv_cur = pl.BlockSpec((tq, KV_WIDTH), cur_map)
    return pl.pallas_call(
        functools.partial(_attn_kernel, nb=nb),
        grid_spec=pltpu.PrefetchScalarGridSpec(
            num_scalar_prefetch=1,
            grid=(n // tq,),
            in_specs=[pl.BlockSpec((tq, ATTN_WIDTH), cur_map),
                      kv_spec(prev_map), kv_cur, kv_spec(next_map),
                      kv_spec(prev_map), kv_cur, kv_spec(next_map)],
            out_specs=pl.BlockSpec((tq, ATTN_WIDTH), cur_map),
            scratch_shapes=[pltpu.VMEM((3, N_KV_HEADS, 3 * BLOCK, GROUP * BLOCK), F32)]),
        out_shape=jax.ShapeDtypeStruct((n, ATTN_WIDTH), BF16),
        compiler_params=pltpu.CompilerParams(
            dimension_semantics=("arbitrary",), vmem_limit_bytes=VMEM_LIMIT),
        name="attn",
    )(sinks, q, k, k, k, v, v, v)


def _merge_kernel(x_ref, a_ref, pc_ref, pp_ref, pn_ref, ga_ref, gp_ref,
                  wab_ref, wpg_ref, ps_ref, wpb_ref, wo_ref, g1_ref, b1_ref, wr_ref, br_ref,
                  h_ref, idx_ref, gate_ref, rank_ref, cnt_ref,
                  ext_ref, carry_ref, earlier_ref, *, seq, tm, tc):
    i = pl.program_id(0)
    tiles_per_seq = seq // tm
    jt = i % tiles_per_seq

    @pl.when(i == 0)
    def _():
        carry_ref[...] = jnp.zeros_like(carry_ref)
        r_i = lax.broadcasted_iota(jnp.int32, (tc, tc), 0)
        c_i = lax.broadcasted_iota(jnp.int32, (tc, tc), 1)
        earlier_ref[...] = (r_i < c_i).astype(BF16)

    ext_ref[0:POOL_HALO, :] = jnp.where(jt > 0, pp_ref[...], 0.0)
    ext_ref[POOL_HALO:POOL_HALO + tm, :] = pc_ref[...]
    ext_ref[POOL_HALO + tm:, :] = jnp.where(jt < tiles_per_seq - 1, pn_ref[...], 0.0)

    w_hi = wr_ref[...].astype(BF16)
    w_lo = (wr_ref[...] - w_hi.astype(F32)).astype(BF16)
    w_split = jnp.concatenate([w_hi, w_lo], axis=0)
    carry = carry_ref[...]

    for r0 in range(0, tm, tc):
        rs = slice(r0, r0 + tc)
        cl = slice(r0, r0 + tc)

        spos = jt * tm + r0 + lax.broadcasted_iota(jnp.int32, (tc, 1), 0)
        pooled_parts = []
        for g, w in enumerate(POOL_WINDOWS):
            cs = slice(g * POOL_GROUP_WIDTH, (g + 1) * POOL_GROUP_WIDTH)
            acc = None
            for off in range(-(w // 2), w // 2):
                lo_row = POOL_HALO + r0 + off
                t = ext_ref[lo_row:lo_row + tc, cs]
                acc = t if acc is None else acc + t
            hi = jnp.minimum(spos + (w // 2 - 1), seq - 1)
            lo = jnp.maximum(spos - w // 2, 0)
            cnt = (hi - lo + 1).astype(F32)
            y = acc / cnt - pc_ref[rs, cs]
            yg = jnp.dot(y.astype(BF16), wpg_ref[g], preferred_element_type=F32)
            pooled_parts.append(yg * ps_ref[:, cs])
        pool_feat = jnp.concatenate(pooled_parts, axis=-1).astype(BF16)

        a_br = jnp.dot(a_ref[rs, :], wab_ref[...], preferred_element_type=F32)
        p_br = jnp.dot(pool_feat, wpb_ref[...], preferred_element_type=F32)
        mixed = ga_ref[rs, :] * a_br + gp_ref[rs, :] * p_br
        o = jnp.dot(mixed.astype(BF16), wo_ref[...], preferred_element_type=F32)
        h = _layer_norm(DEEPNORM_ALPHA * x_ref[rs, :] + o, g1_ref[...], b1_ref[...])
        for c in range(SUBLANES):
            h_ref[pl.ds(r0 * SUBLANES + c, tc, stride=SUBLANES), :] = h[:, c * LANES:(c + 1) * LANES]

        h_hi = h.astype(BF16)
        h_lo = (h - h_hi.astype(F32)).astype(BF16)
        nt = (((1,), (1,)), ((), ()))
        part = lax.dot_general(w_split, h_hi, nt, preferred_element_type=F32)
        logits = (part[:N_EXPERTS] + part[N_EXPERTS:]
                  + lax.dot_general(w_hi, h_lo, nt, preferred_element_type=F32)
                  + br_ref[...])

        eidx = lax.broadcasted_iota(jnp.int32, (N_EXPERTS, tc), 0)
        work = logits
        sel_idx, sel_val, sel_hit = [], [], []
        chosen = jnp.zeros((N_EXPERTS, tc), F32)
        for _ in range(TOP_K):
            m = jnp.max(work, axis=0, keepdims=True)
            ix = jnp.min(jnp.where(work == m, eidx, N_EXPERTS), axis=0, keepdims=True)
            hit = eidx == ix
            sel_idx.append(ix)
            sel_val.append(m)
            sel_hit.append(hit)
            chosen = jnp.where(hit, 1.0, chosen)
            work = jnp.where(hit, -jnp.inf, work)
        ex = [jnp.exp(v - sel_val[0]) for v in sel_val]
        tot = ex[0] + ex[1] + ex[2] + ex[3]
        gate_ref[:, cl] = jnp.concatenate([e / tot for e in ex], axis=0)
        idx_ref[:, cl] = jnp.concatenate(sel_idx, axis=0)

        before = jnp.dot(chosen.astype(BF16), earlier_ref[...], preferred_element_type=F32) + carry
        ranks = [jnp.sum(jnp.where(hit, before, 0.0), axis=0, keepdims=True) for hit in sel_hit]
        rank_ref[:, cl] = jnp.concatenate(ranks, axis=0).astype(jnp.int32)
        carry = carry + jnp.sum(chosen, axis=1, keepdims=True)

    carry_ref[...] = carry
    cnt_ref[...] = carry.astype(jnp.int32)


def _merge(x2, attn, p_in, g_attn, g_pool, wab, wpg, ps, wpb, wo, g1, b1, wr, br, seq):
    n, d = x2.shape
    tm = TM_MERGE
    halo_blocks = tm // POOL_HALO
    row = lambda i: (i, 0)
    col = lambda i: (0, i)
    full2 = lambda i: (0, 0)

    def prev_halo(i):
        return (jnp.maximum(i * halo_blocks - 1, 0), 0)

    def next_halo(i):
        return (jnp.minimum((i + 1) * halo_blocks, n // POOL_HALO - 1), 0)

    return pl.pallas_call(
        functools.partial(_merge_kernel, seq=seq, tm=tm, tc=TC_MERGE),
        grid=(n // tm,),
        in_specs=[pl.BlockSpec((tm, d), row),
                  pl.BlockSpec((tm, ATTN_WIDTH), row),
                  pl.BlockSpec((tm, POOL_WIDTH), row),
                  pl.BlockSpec((POOL_HALO, POOL_WIDTH), prev_halo),
                  pl.BlockSpec((POOL_HALO, POOL_WIDTH), next_halo),
                  pl.BlockSpec((tm, d), row),
                  pl.BlockSpec((tm, d), row),
                  pl.BlockSpec(wab.shape, full2),
                  pl.BlockSpec(wpg.shape, lambda i: (0, 0, 0)),
                  pl.BlockSpec(ps.shape, full2),
                  pl.BlockSpec(wpb.shape, full2),
                  pl.BlockSpec(wo.shape, full2),
                  pl.BlockSpec(g1.shape, full2),
                  pl.BlockSpec(b1.shape, full2),
                  pl.BlockSpec(wr.shape, full2),
                  pl.BlockSpec(br.shape, full2)],
        out_specs=[pl.BlockSpec((tm * SUBLANES, LANES), row),
                   pl.BlockSpec((TOP_K, tm), col),
                   pl.BlockSpec((TOP_K, tm), col),
                   pl.BlockSpec((TOP_K, tm), col),
                   pl.BlockSpec((N_EXPERTS, 1), full2)],
        out_shape=[jax.ShapeDtypeStruct((n * SUBLANES, LANES), F32),
                   jax.ShapeDtypeStruct((TOP_K, n), jnp.int32),
                   jax.ShapeDtypeStruct((TOP_K, n), F32),
                   jax.ShapeDtypeStruct((TOP_K, n), jnp.int32),
                   jax.ShapeDtypeStruct((N_EXPERTS, 1), jnp.int32)],
        scratch_shapes=[pltpu.VMEM((tm + 2 * POOL_HALO, POOL_WIDTH), F32),
                        pltpu.VMEM((N_EXPERTS, 1), F32),
                        pltpu.VMEM((TC_MERGE, TC_MERGE), BF16)],
        compiler_params=pltpu.CompilerParams(
            dimension_semantics=("arbitrary",), vmem_limit_bytes=VMEM_LIMIT),
        name="merge",
    )(x2, attn, p_in, p_in, p_in, g_attn, g_pool, wab, wpg, ps, wpb, wo, g1, b1, wr, br)


def _dispatch_kernel(pos_ref, zfill_ref, h_ref, xs_ref, zero_ref, tile_ref, sem, fsem, zsem,
                     *, tm, nchunks, nsteps):
    i = pl.program_id(0)
    base = i * tm
    n_tokens = nsteps * tm

    @pl.when(pl.program_id(0) == 0)
    def _():
        zero_ref[...] = jnp.zeros_like(zero_ref)

        def flagged_chunks(op):
            def body(c, carry):
                @pl.when(zfill_ref[c] == 1)
                def _():
                    start = pl.multiple_of(c * ZERO_CHUNK, ZERO_CHUNK)
                    op(pltpu.make_async_copy(zero_ref, xs_ref.at[pl.ds(start, ZERO_CHUNK)], zsem))
                return carry
            lax.fori_loop(0, nchunks, body, 0)

        flagged_chunks(lambda c: c.start())
        flagged_chunks(lambda c: c.wait())

    slot = i % 3
    nxt = (i + 1) % 3

    def fetch(step, s):
        src = h_ref.at[pl.ds(pl.multiple_of(step * tm, tm), tm)]
        return pltpu.make_async_copy(src, tile_ref.at[s], fsem.at[s])

    def wait_rows(s):
        for _ in range(TOP_K):
            pltpu.make_async_copy(tile_ref.at[s], xs_ref.at[pl.ds(0, tm)], sem.at[s]).wait()

    @pl.when(i == 0)
    def _():
        fetch(0, 0).start()

    @pl.when(i + 1 < nsteps)
    def _():
        @pl.when(i >= 2)
        def _():
            wait_rows(nxt)
        fetch(i + 1, nxt).start()

    fetch(i, slot).wait()

    def issue(tt, c):
        for u in range(ROW_UNROLL):
            for k in range(TOP_K):
                t = tt * ROW_UNROLL + u
                dst = pos_ref[k * n_tokens + base + t]
                pltpu.make_async_copy(tile_ref.at[slot, t], xs_ref.at[dst],
                                      sem.at[slot]).start(priority=k % 2)
        return c

    lax.fori_loop(0, tm // ROW_UNROLL, issue, 0)

    @pl.when(i == nsteps - 1)
    def _():
        for s in range(3):
            wait_rows(s)


def _dispatch(pos_flat, chunk_has_padding, h_tiles, n_rows):
    n = h_tiles.shape[0]
    tm = TM_ROWS
    assert n // tm >= 3, "the staging ring assumes at least three token tiles"
    return pl.pallas_call(
        functools.partial(_dispatch_kernel, tm=tm, nchunks=n_rows // ZERO_CHUNK, nsteps=n // tm),
        grid_spec=pltpu.PrefetchScalarGridSpec(
            num_scalar_prefetch=2,
            grid=(n // tm,),
            in_specs=[pl.BlockSpec(memory_space=pl.ANY)],
            out_specs=pl.BlockSpec(memory_space=pl.ANY),
            scratch_shapes=[pltpu.VMEM((ZERO_CHUNK, SUBLANES, LANES), F32),
                            pltpu.VMEM((3, tm, SUBLANES, LANES), F32),
                            pltpu.SemaphoreType.DMA((3,)),
                            pltpu.SemaphoreType.DMA((3,)),
                            pltpu.SemaphoreType.DMA]),
        out_shape=jax.ShapeDtypeStruct((n_rows, SUBLANES, LANES), F32),
        compiler_params=pltpu.CompilerParams(
            dimension_semantics=("arbitrary",), vmem_limit_bytes=VMEM_LIMIT),
        name="dispatch",
    )(pos_flat, chunk_has_padding, h_tiles)


def _ffn_kernel(be_ref, nv_ref, first_ref, nxt_ref, nu_ref,
                x_ref, w1_hbm, b1_ref, w2_hbm, b2_ref, y_ref,
                w1f_ref, w2f_ref, w1b_ref, w2b_ref, wsem, *, bm, f):
    b = pl.program_id(0)
    used = b < nu_ref[0]

    def fetch(e):
        return (pltpu.make_async_copy(w1_hbm.at[e], w1f_ref, wsem.at[0]),
                pltpu.make_async_copy(w2_hbm.at[e], w2f_ref, wsem.at[1]))

    @pl.when(b == 0)
    def _():
        for c in fetch(be_ref[0]):
            c.start()

    @pl.when(used & (first_ref[b] == 1))
    def _():
        for c in fetch(be_ref[b]):
            c.wait()
        w1b_ref[...] = w1f_ref[...].astype(BF16)
        w2b_ref[...] = w2f_ref[...].astype(BF16)

        @pl.when(nxt_ref[b] >= 0)
        def _():
            for c in fetch(nxt_ref[b]):
                c.start()

    def mlp(rows):
        ridx = lax.broadcasted_iota(jnp.int32, (rows, 1), 0)
        x = jnp.where(ridx < nv_ref[b], _load_row_tiles(x_ref, rows), 0.0).astype(BF16)
        h = jnp.dot(x, w1b_ref[...], preferred_element_type=F32) + b1_ref[0]
        gate = jnp.minimum(h[:, :f], SWIGLU_LIMIT)
        up = jnp.clip(h[:, f:], -SWIGLU_LIMIT, SWIGLU_LIMIT)
        act = (up + 1.0) * gate * _sigmoid(SWIGLU_ALPHA * gate)
        y = jnp.dot(act.astype(BF16), w2b_ref[...], preferred_element_type=F32) + b2_ref[0]
        _store_row_tiles(y_ref, y)

    for rows in range(FFN_ROW_STEP, bm + 1, FFN_ROW_STEP):
        @pl.when(used & (nv_ref[b] > rows - FFN_ROW_STEP) & (nv_ref[b] <= rows))
        def _(rows=rows):
            mlp(rows)
            if rows < bm:
                y_ref[rows * SUBLANES:, :] = jnp.zeros(((bm - rows) * SUBLANES, LANES), F32)

    @pl.when(b >= nu_ref[0])
    def _():
        y_ref[...] = jnp.zeros_like(y_ref)


def _ffn(blk_expert, blk_valid, blk_first, blk_next, n_used, xs_tiles, w1, b1, w2, b2):
    n_rows = xs_tiles.shape[0] // SUBLANES
    bm = BM_FFN
    d, f = w1.shape[1], w2.shape[1]

    def row_map(b, be, nv, fi, nx, nu):
        return (jnp.minimum(b, nu[0] - 1), 0)

    def exp_map(b, be, nv, fi, nx, nu):
        return (be[b], 0, 0)

    return pl.pallas_call(
        functools.partial(_ffn_kernel, bm=bm, f=f),
        grid_spec=pltpu.PrefetchScalarGridSpec(
            num_scalar_prefetch=5,
            grid=(n_rows // bm,),
            in_specs=[pl.BlockSpec((bm * SUBLANES, LANES), row_map),
                      pl.BlockSpec(memory_space=pl.ANY),
                      pl.BlockSpec((1, 1, 2 * f), exp_map),
                      pl.BlockSpec(memory_space=pl.ANY),
                      pl.BlockSpec((1, 1, d), exp_map)],
            out_specs=pl.BlockSpec((bm * SUBLANES, LANES), lambda b, *_: (b, 0)),
            scratch_shapes=[pltpu.VMEM((d, 2 * f), F32),
                            pltpu.VMEM((f, d), F32),
                            pltpu.VMEM((d, 2 * f), BF16),
                            pltpu.VMEM((f, d), BF16),
                            pltpu.SemaphoreType.DMA((2,))]),
        out_shape=jax.ShapeDtypeStruct((n_rows * SUBLANES, LANES), F32),
        compiler_params=pltpu.CompilerParams(
            dimension_semantics=("arbitrary",), vmem_limit_bytes=VMEM_LIMIT),
        name="ffn",
    )(blk_expert, blk_valid, blk_first, blk_next, n_used, xs_tiles, w1, b1, w2, b2)


def _combine_kernel(pos_ref, h_ref, gate_ref, g2_ref, b2_ref, ys_ref, o_ref, buf_ref, sem, *, tm, nsteps):
    i = pl.program_id(0)
    slot = i % 2
    ch = COMBINE_CHUNK

    def request_rows(step, s, t0, count):
        base = step * tm
        for u in range(count):
            for k in range(TOP_K):
                src = pos_ref[k * (nsteps * tm) + base + t0 + u]
                pltpu.make_async_copy(ys_ref.at[src], buf_ref.at[s, k, t0 + u],
                                      sem.at[s]).start(priority=k % 2)

    def compute_rows(r0):
        def rows_of(ref):
            return jnp.concatenate(
                [ref[pl.ds(r0 * SUBLANES + c, ch, stride=SUBLANES), :] for c in range(SUBLANES)], axis=-1)

        def gathered(k):
            return buf_ref.at[slot, k].reshape(tm * SUBLANES, LANES)

        gates = gate_ref[pl.ds(r0, ch), :]
        f = gates[:, 0:1] * rows_of(gathered(0))
        for k in range(1, TOP_K):
            f = f + gates[:, k:k + 1] * rows_of(gathered(k))
        o_ref[pl.ds(r0, ch), :] = _layer_norm(DEEPNORM_ALPHA * rows_of(h_ref) + f, g2_ref[...], b2_ref[...])

    @pl.when(i == 0)
    def _():
        def issue(c, carry):
            request_rows(0, 0, c * ROW_UNROLL, ROW_UNROLL)
            return carry
        lax.fori_loop(0, tm // ROW_UNROLL, issue, 0)

    for k in range(TOP_K):
        pltpu.make_async_copy(ys_ref.at[pl.ds(0, tm)], buf_ref.at[slot, k], sem.at[slot]).wait()

    @pl.when(i + 1 < nsteps)
    def _():
        def body(c, carry):
            r0 = pl.multiple_of(c * ch, ch)
            request_rows(i + 1, 1 - slot, r0, ch)
            compute_rows(r0)
            return carry
        lax.fori_loop(0, tm // ch, body, 0)

    @pl.when(i + 1 >= nsteps)
    def _():
        def body(c, carry):
            compute_rows(pl.multiple_of(c * ch, ch))
            return carry
        lax.fori_loop(0, tm // ch, body, 0)


def _combine(pos_flat, h_tiles, gates, g2, b2, ys_tiles):
    n = h_tiles.shape[0] // SUBLANES
    d = SUBLANES * LANES
    tm = TM_ROWS
    row = lambda i, p: (i, 0)
    full2 = lambda i, p: (0, 0)
    return pl.pallas_call(
        functools.partial(_combine_kernel, tm=tm, nsteps=n // tm),
        grid_spec=pltpu.PrefetchScalarGridSpec(
            num_scalar_prefetch=1,
            grid=(n // tm,),
            in_specs=[pl.BlockSpec((tm * SUBLANES, LANES), row),
                      pl.BlockSpec((tm, TOP_K), row),
                      pl.BlockSpec(g2.shape, full2),
                      pl.BlockSpec(b2.shape, full2),
                      pl.BlockSpec(memory_space=pl.ANY)],
            out_specs=pl.BlockSpec((tm, d), row),
            scratch_shapes=[pltpu.VMEM((2, TOP_K, tm, SUBLANES, LANES), F32),
                            pltpu.SemaphoreType.DMA((2,))]),
        out_shape=jax.ShapeDtypeStruct((n, d), F32),
        compiler_params=pltpu.CompilerParams(
            dimension_semantics=("arbitrary",), vmem_limit_bytes=VMEM_LIMIT),
        name="combine",
    )(pos_flat, h_tiles, gates, g2, b2, ys_tiles)


def _layer(h2, seq, w_in, sinks, wab, wpg, ps, wpb, wo, g1, b1, wr, br, w1, bb1, w2, bb2, g2, b2):
    n, d = h2.shape
    q, k, v, p_in, g_attn, g_pool = _inproj(h2, w_in.astype(BF16))
    attn = _attention(q, k, v, sinks, seq)
    h1, idx, gates, rank, counts = _merge(
        h2, attn, p_in, g_attn, g_pool,
        wab.astype(BF16), wpg.astype(BF16), ps.reshape(1, -1), wpb.astype(BF16), wo.astype(BF16),
        g1.reshape(1, -1), b1.reshape(1, -1), wr.T, br.reshape(-1, 1), seq)
    gates = gates.T

    bm = BM_FFN
    counts = counts.reshape(N_EXPERTS)
    experts = jnp.arange(N_EXPERTS, dtype=jnp.int32)
    upto = experts[None, :] <= experts[:, None]

    def prefix_sum(v):
        return jnp.sum(jnp.where(upto, v[None, :], 0), axis=1)

    padded = ((counts + bm - 1) // bm) * bm
    pend = prefix_sum(padded)
    pstart = pend - padded
    n_rows = n * TOP_K + N_EXPERTS * bm
    nblk = n_rows // bm
    blk_lo = jnp.arange(nblk, dtype=jnp.int32) * bm
    blk_expert = jnp.minimum(
        jnp.sum((blk_lo[:, None] >= pend[None, :]).astype(jnp.int32), axis=1), N_EXPERTS - 1)
    blk_is = blk_expert[:, None] == experts[None, :]

    def per_block(table):
        return jnp.sum(jnp.where(blk_is, table[None, :], 0), axis=1).astype(jnp.int32)

    blk_valid = jnp.clip(per_block(pstart + counts) - blk_lo, 0, bm).astype(jnp.int32)
    per_blk = bm // ZERO_CHUNK
    chunk_end_in_blk = (jnp.arange(per_blk, dtype=jnp.int32) + 1) * ZERO_CHUNK
    chunk_has_padding = (chunk_end_in_blk[None, :] > blk_valid[:, None]).astype(jnp.int32).reshape(-1)
    n_used = (pend[-1:] // bm).astype(jnp.int32)
    nonempty = counts > 0
    later = jnp.where(nonempty[None, :] & ~upto, experts[None, :], N_EXPERTS)
    next_group = jnp.min(later, axis=1)
    next_group = jnp.where(next_group == N_EXPERTS, -1, next_group)
    blk_first = (blk_lo == per_block(pstart)).astype(jnp.int32)
    blk_next = per_block(next_group)

    onehot = idx[None, :, :] == experts[:, None, None]
    pos = rank + jnp.sum(jnp.where(onehot, pstart[:, None, None], 0), axis=0)
    pos_flat = pos.reshape(TOP_K * n).astype(jnp.int32)

    xs = _dispatch(pos_flat, chunk_has_padding, h1.reshape(n, SUBLANES, LANES), n_rows)
    ys = _ffn(blk_expert.astype(jnp.int32), blk_valid, blk_first, blk_next, n_used,
              xs.reshape(n_rows * SUBLANES, LANES),
              w1, bb1.reshape(N_EXPERTS, 1, -1), w2, bb2.reshape(N_EXPERTS, 1, -1))
    return _combine(pos_flat, h1, gates, g2.reshape(1, -1), b2.reshape(1, -1),
                    ys.reshape(n_rows, SUBLANES, LANES))


def kernel(x, w_in, attn_sinks, w_attn_branch, w_pool_group, pool_scale, w_pool_branch, w_out,
           ln1_g, ln1_b, w_router, b_router, w_mlp1, b_mlp1, w_mlp2, b_mlp2, ln2_g, ln2_b):
    bsz, seq, d = x.shape
    assert d == SUBLANES * LANES, "row-tile layout needs one (8, 128) f32 tile per token row"
    h = x.reshape(bsz * seq, d)
    for l in range(w_in.shape[0]):
        h = _layer(h, seq, w_in[l], attn_sinks[l], w_attn_branch[l], w_pool_group[l], pool_scale[l],
                   w_pool_branch[l], w_out[l], ln1_g[l], ln1_b[l], w_router[l], b_router[l],
                   w_mlp1[l], b_mlp1[l], w_mlp2[l], b_mlp2[l], ln2_g[l], ln2_b[l])
    return h.reshape(bsz, seq, d)
```
